```python
import math
import jax, jax.numpy as jnp
from jax import lax
import numpy as np

D_MODEL = 1024
BATCH = 16
SEQ = 4096
DEPTH = 1
DEC_BATCH = 8
DEC_SEQ = 2048
PAST_LEN = 128

HEAD_DIM = 128
N_Q_HEADS = 8
N_KV_HEADS = 2
GROUP = N_Q_HEADS // N_KV_HEADS
Q_WIDTH = N_Q_HEADS * HEAD_DIM
KV_WIDTH = N_KV_HEADS * HEAD_DIM
WINDOW = 128
BLOCK = 128
ROT_DIM = HEAD_DIM // 4
ROPE_THETA = 500000.0
NEG_BIG = -1e30
LRU_WIDTH = 1280
LRU_BLOCKS = 10
LRU_BLOCK_DIM = LRU_WIDTH // LRU_BLOCKS
CONV_WIDTH = 4
CONV_PAD_LEFT = 2
LRU_C = 8.0
N_DIRS = 2
D_FF = ((8 * D_MODEL // 3 + 255) // 256) * 256
EPS = 1e-6
IN_WIDTH = Q_WIDTH + 2 * KV_WIDTH + 2 * LRU_WIDTH + 2 * D_MODEL

kernel_name = "hybrid_swa_rglru_parallel_encoder"


def rmsnorm(x, w):
    x32 = x.astype(jnp.float32)
    y = x32 * lax.rsqrt(jnp.mean(x32 * x32, axis=-1, keepdims=True) + EPS)
    return (y * w.astype(jnp.float32)).astype(x.dtype)


def partial_rope(x, pos):
    half = ROT_DIM // 2
    inv = ROPE_THETA ** (-jnp.arange(half, dtype=jnp.float32) / half)
    ang = pos.astype(jnp.float32)[:, None] * inv[None, :]
    cos = jnp.cos(ang)[None, :, None, :]
    sin = jnp.sin(ang)[None, :, None, :]
    xr = x[..., :ROT_DIM].astype(jnp.float32)
    x1, x2 = xr[..., :half], xr[..., half:]
    rot = jnp.concatenate([x1 * cos - x2 * sin, x2 * cos + x1 * sin], axis=-1)
    return jnp.concatenate([rot.astype(x.dtype), x[..., ROT_DIM:]], axis=-1)


def local_attention(q, k, v, sink):
    B, S = q.shape[0], q.shape[1]
    nb = S // BLOCK
    qb = q.reshape(B, nb, BLOCK, N_KV_HEADS, GROUP, HEAD_DIM)

    def neighbours(t):
        tb = t.reshape(B, nb, BLOCK, N_KV_HEADS, HEAD_DIM)
        tp = jnp.pad(tb, ((0, 0), (1, 1), (0, 0), (0, 0), (0, 0)))
        return jnp.concatenate([tp[:, :-2], tp[:, 1:-1], tp[:, 2:]], axis=2)

    kn, vn = neighbours(k), neighbours(v)
    scale = 1.0 / math.sqrt(HEAD_DIM)
    s = jnp.einsum('bnqkgd,bnskd->bnkgqs', qb, kn).astype(jnp.float32) * scale
    blk = jnp.arange(nb)[:, None, None]
    qpos = blk * BLOCK + jnp.arange(BLOCK)[None, :, None]
    kpos = (blk - 1) * BLOCK + jnp.arange(3 * BLOCK)[None, None, :]
    valid = (jnp.abs(kpos - qpos) <= WINDOW) & (kpos >= 0) & (kpos < S)
    s = jnp.where(valid[None, :, None, None], s, NEG_BIG)
    sink_l = sink.astype(jnp.float32).reshape(N_KV_HEADS, GROUP)[None, None, :, :, None, None]
    m = jnp.maximum(jnp.max(s, axis=-1, keepdims=True), sink_l)
    p = jnp.exp(s - m)
    p = p / (jnp.sum(p, axis=-1, keepdims=True) + jnp.exp(sink_l - m))
    o = jnp.einsum('bnkgqs,bnskd->bnqkgd', p.astype(v.dtype), vn)
    return o.reshape(B, S, Q_WIDTH)


def centred_conv(x, w, b):
    S = x.shape[1]
    xp = jnp.pad(x, ((0, 0), (CONV_PAD_LEFT, CONV_WIDTH - 1 - CONV_PAD_LEFT), (0, 0)))
    y = b + xp[:, 0:S] * w[0]
    for t in range(1, CONV_WIDTH):
        y = y + xp[:, t:t + S] * w[t]
    return y


def rglru_coeffs(xc, w_a, b_a, w_i, b_i, lam):
    B, S, _ = xc.shape
    xb = xc.reshape(B, S, LRU_BLOCKS, LRU_BLOCK_DIM)
    ga = jnp.einsum('bsnc,ncd->bsnd', xb, w_a).reshape(B, S, LRU_WIDTH) + b_a
    gi = jnp.einsum('bsnc,ncd->bsnd', xb, w_i).reshape(B, S, LRU_WIDTH) + b_i
    r = jax.nn.sigmoid(ga.astype(jnp.float32))
    i = jax.nn.sigmoid(gi.astype(jnp.float32))
    log_a = -LRU_C * r * jax.nn.softplus(-lam.astype(jnp.float32))
    a = jnp.exp(log_a)
    u = jnp.sqrt(-jnp.expm1(2.0 * log_a)) * (i * xc.astype(jnp.float32))
    return a, u


def linear_scan(a, u):
    def combine(l, r):
        a1, b1 = l
        a2, b2 = r
        return a1 * a2, a2 * b1 + b2
    _, h = lax.associative_scan(combine, (a, u), axis=1)
    return h


def bidir_rglru(xc, w_a, b_a, w_i, b_i, lam):
    a_f, u_f = rglru_coeffs(xc, w_a[0], b_a[0], w_i[0], b_i[0], lam[0])
    a_b, u_b = rglru_coeffs(xc, w_a[1], b_a[1], w_i[1], b_i[1], lam[1])
    h_f = linear_scan(a_f, u_f)
    h_b = jnp.flip(linear_scan(jnp.flip(a_b, 1), jnp.flip(u_b, 1)), 1)
    return (h_f + h_b).astype(xc.dtype)


def encoder_layer(x, norm_mix_pre, w_in, attn_sink, conv_w, conv_b, lru_w_a, lru_b_a,
                  lru_w_i, lru_b_i, lru_lambda, w_attn_proj, w_rec_proj, w_out,
                  norm_mix_post, norm_ffn_pre, w_ffn_in, w_ffn_out, norm_ffn_post):
    B, S, _ = x.shape
    h = rmsnorm(x, norm_mix_pre)
    z = h @ w_in
    o1 = Q_WIDTH
    o2 = o1 + KV_WIDTH
    o3 = o2 + KV_WIDTH
    o4 = o3 + LRU_WIDTH
    o5 = o4 + LRU_WIDTH
    o6 = o5 + D_MODEL
    q, k, v = z[..., :o1], z[..., o1:o2], z[..., o2:o3]
    rec_x, rec_gate = z[..., o3:o4], z[..., o4:o5]
    g_attn, g_rec = z[..., o5:o6], z[..., o6:]
    pos = jnp.arange(S)
    q = partial_rope(q.reshape(B, S, N_Q_HEADS, HEAD_DIM), pos)
    k = partial_rope(k.reshape(B, S, N_KV_HEADS, HEAD_DIM), pos)
    v = v.reshape(B, S, N_KV_HEADS, HEAD_DIM)
    attn = local_attention(q, k, v, attn_sink)
    xc = centred_conv(rec_x, conv_w, conv_b)
    rec = bidir_rglru(xc, lru_w_a, lru_b_a, lru_w_i, lru_b_i, lru_lambda) * jax.nn.gelu(rec_gate)
    merged = jax.nn.sigmoid(g_attn) * (attn @ w_attn_proj) + jax.nn.sigmoid(g_rec) * (rec @ w_rec_proj)
    x = x + rmsnorm(merged @ w_out, norm_mix_post)
    h = rmsnorm(x, norm_ffn_pre)
    gu = h @ w_ffn_in
    f = (jax.nn.silu(gu[..., :D_FF]) * gu[..., D_FF:]) @ w_ffn_out
    return x + rmsnorm(f, norm_ffn_post)


def setup_inputs(seed: int = 0) -> dict:
    key = jax.random.key(seed)
    ks = jax.random.split(key, 24)
    f32 = jnp.float32

    def nrm(k, shape, fan_in):
        return jax.random.normal(k, shape, f32) * (fan_in ** -0.5)

    def gain(k):
        return 1.0 + 0.05 * jax.random.normal(k, (DEPTH, D_MODEL), f32)

    a0 = jax.random.uniform(ks[12], (DEPTH, N_DIRS, LRU_WIDTH), f32, 0.9, 0.999)
    sig = a0 ** (1.0 / LRU_C)
    lam = jnp.log(sig) - jnp.log1p(-sig)
    return {
        "x_prompt": jax.random.normal(ks[0], (BATCH, SEQ, D_MODEL), f32),
        "x_sample": jax.random.normal(ks[1], (DEC_BATCH, DEC_SEQ, D_MODEL), f32),
        "norm_mix_pre": gain(ks[2]),
        "w_in": nrm(ks[3], (DEPTH, D_MODEL, IN_WIDTH), D_MODEL),
        "attn_sink": 0.5 * jax.random.normal(ks[4], (DEPTH, N_Q_HEADS), f32),
        "conv_w": nrm(ks[5], (DEPTH, CONV_WIDTH, LRU_WIDTH), CONV_WIDTH),
        "conv_b": 0.02 * jax.random.normal(ks[6], (DEPTH, LRU_WIDTH), f32),
        "lru_w_a": nrm(ks[7], (DEPTH, N_DIRS, LRU_BLOCKS, LRU_BLOCK_DIM, LRU_BLOCK_DIM), LRU_BLOCK_DIM),
        "lru_b_a": 0.02 * jax.random.normal(ks[8], (DEPTH, N_DIRS, LRU_WIDTH), f32),
        "lru_w_i": nrm(ks[9], (DEPTH, N_DIRS, LRU_BLOCKS, LRU_BLOCK_DIM, LRU_BLOCK_DIM), LRU_BLOCK_DIM),
        "lru_b_i": 0.02 * jax.random.normal(ks[10], (DEPTH, N_DIRS, LRU_WIDTH), f32),
        "lru_lambda": lam,
        "w_attn_proj": nrm(ks[13], (DEPTH, Q_WIDTH, D_MODEL), Q_WIDTH),
        "w_rec_proj": nrm(ks[14], (DEPTH, LRU_WIDTH, D_MODEL), LRU_WIDTH),
        "w_out": nrm(ks[15], (DEPTH, D_MODEL, D_MODEL), D_MODEL),
        "norm_mix_post": gain(ks[16]),
        "norm_ffn_pre": gain(ks[17]),
        "w_ffn_in": nrm(ks[18], (DEPTH, D_MODEL, 2 * D_FF), D_MODEL),
        "w_ffn_out": nrm(ks[19], (DEPTH, D_FF, D_MODEL), D_FF),
        "norm_ffn_post": gain(ks[20]),
    }


def reference(x_prompt, x_sample, norm_mix_pre, w_in, attn_sink, conv_w, conv_b, lru_w_a,
              lru_b_a, lru_w_i, lru_b_i, lru_lambda, w_attn_proj, w_rec_proj, w_out,
              norm_mix_post, norm_ffn_pre, w_ffn_in, w_ffn_out, norm_ffn_post):
    y_prompt = x_prompt
    y_sample = x_sample
    for l in range(DEPTH):
        p = (norm_mix_pre[l], w_in[l], attn_sink[l], conv_w[l], conv_b[l], lru_w_a[l],
             lru_b_a[l], lru_w_i[l], lru_b_i[l], lru_lambda[l], w_attn_proj[l], w_rec_proj[l],
             w_out[l], norm_mix_post[l], norm_ffn_pre[l], w_ffn_in[l], w_ffn_out[l],
             norm_ffn_post[l])
        y_prompt = encoder_layer(y_prompt, *p)
        y_sample = encoder_layer(y_sample, *p)
    return (y_prompt, y_sample)
```

```python
import functools
import math

import jax
import jax.numpy as jnp
from jax import lax
from jax.experimental import pallas as pl
from jax.experimental.pallas import tpu as pltpu

D_MODEL = 1024
HEAD_DIM = 128
N_Q_HEADS = 8
N_KV_HEADS = 2
GROUP = N_Q_HEADS // N_KV_HEADS
Q_WIDTH = N_Q_HEADS * HEAD_DIM
KV_WIDTH = N_KV_HEADS * HEAD_DIM
WINDOW = 128
ROT_DIM = HEAD_DIM // 4
ROPE_THETA = 500000.0
NEG_BIG = -1e30
LRU_WIDTH = 1280
LRU_BLOCKS = 10
LRU_BLOCK_DIM = 128
LRU_C = 8.0
D_FF = 2816
EPS = 1e-6
IN_WIDTH = Q_WIDTH + 2 * KV_WIDTH + 2 * LRU_WIDTH + 2 * D_MODEL
O_Q = 0
O_K = O_Q + Q_WIDTH
O_V = O_K + KV_WIDTH
O_RX = O_V + KV_WIDTH
O_RG = O_RX + LRU_WIDTH
O_GA = O_RG + LRU_WIDTH
O_GR = O_GA + D_MODEL

LANES = 128
SUBLANES = 8
BATCH_GROUP = SUBLANES
VMEM_LIMIT = 56 * 1024 * 1024

BF16 = jnp.bfloat16
F32 = jnp.float32


def _const_spec(shape):
    nd = len(shape)
    return pl.BlockSpec(shape, lambda *_: (0,) * nd, pipeline_mode=pl.Buffered(1))


def _sigmoid(x):
    return 0.5 * jnp.tanh(0.5 * x) + 0.5


def _rms(x, w):
    ms = jnp.mean(x * x, axis=-1, keepdims=True)
    return x * lax.rsqrt(ms + EPS) * w


def _in_proj_kernel(x_ref, nw_ref, w_ref, rc_ref, rs1_ref, rs2_ref,
                    q_ref, k_ref, v_ref, xr_ref, rg_ref, ga_ref, gr_ref, *, tt):
    rows = BATCH_GROUP * tt
    x = x_ref[...].reshape(rows, D_MODEL)
    h = _rms(x, nw_ref[...]).astype(BF16)

    def proj(lo, width):
        return jnp.dot(h, w_ref[:, lo:lo + width], preferred_element_type=F32)

    rc = rc_ref[...][None]
    rs1 = rs1_ref[...][None]
    rs2 = rs2_ref[...][None]

    def rope(t):
        up = pltpu.roll(t, ROT_DIM // 2, axis=1).reshape(BATCH_GROUP, tt, HEAD_DIM)
        dn = pltpu.roll(t, HEAD_DIM - ROT_DIM // 2, axis=1).reshape(BATCH_GROUP, tt, HEAD_DIM)
        return t.reshape(BATCH_GROUP, tt, HEAD_DIM) * rc + up * rs1 + dn * rs2

    scale = 1.0 / math.sqrt(HEAD_DIM)
    zq = proj(O_Q, Q_WIDTH)
    for hh in range(N_Q_HEADS):
        sl = slice(hh * HEAD_DIM, (hh + 1) * HEAD_DIM)
        q_ref[:, :, sl] = (rope(zq[:, sl]) * scale).astype(BF16)
    zk = proj(O_K, KV_WIDTH)
    for hh in range(N_KV_HEADS):
        sl = slice(hh * HEAD_DIM, (hh + 1) * HEAD_DIM)
        k_ref[:, :, sl] = rope(zk[:, sl]).astype(BF16)
    v_ref[...] = proj(O_V, KV_WIDTH).astype(BF16).reshape(BATCH_GROUP, tt, KV_WIDTH)

    zx = proj(O_RX, LRU_WIDTH)
    for n in range(LRU_BLOCKS):
        for b in range(BATCH_GROUP):
            xr_ref[0, n, pl.ds(b, tt, stride=BATCH_GROUP), :] = (
                zx[b * tt:(b + 1) * tt, n * LANES:(n + 1) * LANES])

    rg_ref[...] = jax.nn.gelu(proj(O_RG, LRU_WIDTH)).astype(BF16).reshape(BATCH_GROUP, tt, LRU_WIDTH)
    ga_ref[...] = _sigmoid(proj(O_GA, D_MODEL)).astype(BF16).reshape(BATCH_GROUP, tt, D_MODEL)
    gr_ref[...] = _sigmoid(proj(O_GR, D_MODEL)).astype(BF16).reshape(BATCH_GROUP, tt, D_MODEL)


def _in_proj(x, norm_w, w_in, rope_c, rope_s1, rope_s2, *, tt):
    B, S, _ = x.shape
    nbg = B // BATCH_GROUP
    grid = (nbg, S // tt)

    def bm(width):
        return pl.BlockSpec((BATCH_GROUP, tt, width), lambda g, i: (g, i, 0))

    rope_spec = pl.BlockSpec((tt, HEAD_DIM), lambda g, i: (i, 0))
    out_shape = (
        jax.ShapeDtypeStruct((B, S, Q_WIDTH), BF16),
        jax.ShapeDtypeStruct((B, S, KV_WIDTH), BF16),
        jax.ShapeDtypeStruct((B, S, KV_WIDTH), BF16),
        jax.ShapeDtypeStruct((nbg, LRU_BLOCKS, S * BATCH_GROUP, LANES), F32),
        jax.ShapeDtypeStruct((B, S, LRU_WIDTH), BF16),
        jax.ShapeDtypeStruct((B, S, D_MODEL), BF16),
        jax.ShapeDtypeStruct((B, S, D_MODEL), BF16),
    )
    out_specs = (
        bm(Q_WIDTH), bm(KV_WIDTH), bm(KV_WIDTH),
        pl.BlockSpec((1, LRU_BLOCKS, tt * BATCH_GROUP, LANES), lambda g, i: (g, 0, i, 0)),
        bm(LRU_WIDTH), bm(D_MODEL), bm(D_MODEL),
    )
    return pl.pallas_call(
        functools.partial(_in_proj_kernel, tt=tt),
        grid=grid,
        in_specs=[bm(D_MODEL), _const_spec((1, D_MODEL)), _const_spec((D_MODEL, IN_WIDTH)),
                  rope_spec, rope_spec, rope_spec],
        out_specs=out_specs,
        out_shape=out_shape,
        compiler_params=pltpu.CompilerParams(
            dimension_semantics=("parallel", "parallel"), vmem_limit_bytes=VMEM_LIMIT),
        name="in_proj",
    )(x, norm_w, w_in, rope_c, rope_s1, rope_s2)


def _attn_kernel(sink_ref, q_ref, kp_ref, km_ref, kn_ref, vp_ref, vm_ref, vn_ref, o_ref, *, tq, nblk):
    i = pl.program_id(1)
    nsub = tq // WINDOW
    kcat = jnp.concatenate([kp_ref[0], km_ref[0], kn_ref[0]], axis=0)
    vcat = jnp.concatenate([vp_ref[0], vm_ref[0], vn_ref[0]], axis=0)
    rows = GROUP * WINDOW
    r = lax.broadcasted_iota(jnp.int32, (rows, 3 * WINDOW), 0) % WINDOW
    c = lax.broadcasted_iota(jnp.int32, (rows, 3 * WINDOW), 1)
    head_of_row = lax.broadcasted_iota(jnp.int32, (rows, 1), 0) // WINDOW
    for j in range(nsub):
        blk = i * nsub + j
        lo = jnp.where(blk == 0, WINDOW, 0)
        hi = jnp.where(blk == nblk - 1, 2 * WINDOW - 1, 3 * WINDOW - 1)
        lower = jnp.maximum(r, lo)
        upper = jnp.minimum(r + 2 * WINDOW, hi)
        for g in range(N_KV_HEADS):
            ks = kcat[j * WINDOW:(j + 3) * WINDOW, g * HEAD_DIM:(g + 1) * HEAD_DIM]
            vs = vcat[j * WINDOW:(j + 3) * WINDOW, g * HEAD_DIM:(g + 1) * HEAD_DIM]
            qs = jnp.concatenate(
                [q_ref[0, j * WINDOW:(j + 1) * WINDOW,
                       (g * GROUP + hh) * HEAD_DIM:(g * GROUP + hh + 1) * HEAD_DIM]
                 for hh in range(GROUP)], axis=0)
            s = lax.dot_general(qs, ks, (((1,), (1,)), ((), ())), preferred_element_type=F32)
            s = jnp.where(c >= lower, jnp.where(c <= upper, s, NEG_BIG), NEG_BIG)
            sink = jnp.zeros((rows, 1), F32)
            for hh in range(GROUP):
                sink = jnp.where(head_of_row == hh, sink_ref[g * GROUP + hh], sink)
            m = jnp.maximum(jnp.max(s, axis=-1, keepdims=True), sink)
            p = jnp.exp(s - m)
            denom = jnp.sum(p, axis=-1, keepdims=True) + jnp.exp(sink - m)
            o = jnp.dot(p.astype(BF16), vs, preferred_element_type=F32) / denom
            for hh in range(GROUP):
                col = (g * GROUP + hh) * HEAD_DIM
                o_ref[0, j * WINDOW:(j + 1) * WINDOW, col:col + HEAD_DIM] = (
                    o[hh * WINDOW:(hh + 1) * WINDOW].astype(BF16))


def _attention(q, k, v, sink, *, tq):
    B, S, _ = q.shape
    nblk = S // WINDOW
    nsub = tq // WINDOW
    grid = (B, S // tq)
    main = pl.BlockSpec((1, tq, KV_WIDTH), lambda b, i: (b, i, 0))
    prev = pl.BlockSpec((1, WINDOW, KV_WIDTH), lambda b, i: (b, jnp.maximum(i * nsub - 1, 0), 0))
    nxt = pl.BlockSpec((1, WINDOW, KV_WIDTH),
                       lambda b, i: (b, jnp.minimum((i + 1) * nsub, nblk - 1), 0))
    qspec = pl.BlockSpec((1, tq, Q_WIDTH), lambda b, i: (b, i, 0))
    return pl.pallas_call(
        functools.partial(_attn_kernel, tq=tq, nblk=nblk),
        grid=grid,
        in_specs=[pl.BlockSpec(memory_space=pltpu.SMEM), qspec, prev, main, nxt, prev, main, nxt],
        out_specs=qspec,
        out_shape=jax.ShapeDtypeStruct((B, S, Q_WIDTH), BF16),
        compiler_params=pltpu.CompilerParams(
            dimension_semantics=("parallel", "parallel"), vmem_limit_bytes=VMEM_LIMIT),
        name="attention",
    )(sink, q, k, k, k, v, v, v)


SLABS = 2


def _rec_kernel(xm_ref, xp_ref, xn_ref, cw_ref, cb_ref, wg_ref, bg_ref, lam_ref, gate_ref,
                o_ref, hf_ref, a_ref, u_ref, rl_ref, carry_ref, *, tc, nchunk):
    j = pl.program_id(2)
    backward = j >= nchunk
    chunk = jnp.where(backward, 2 * nchunk - 1 - j, j)
    rows = tc * BATCH_GROUP

    @pl.when((j == 0) | (j == nchunk))
    def _():
        carry_ref[...] = jnp.zeros_like(carry_ref)

    has_prev = (chunk > 0).astype(F32)
    has_next = (chunk < nchunk - 1).astype(F32)
    for s in range(SLABS):
        x = xm_ref[0, s]
        xp = xp_ref[0, s] * has_prev
        xn = xn_ref[0, s] * has_next
        cw = cw_ref[s]
        xm2 = jnp.concatenate([xp, x[:rows - 2 * BATCH_GROUP]], axis=0)
        xm1 = jnp.concatenate([xp[BATCH_GROUP:], x[:rows - BATCH_GROUP]], axis=0)
        xp1 = jnp.concatenate([x[BATCH_GROUP:], xn], axis=0)
        xc = cb_ref[s] + xm2 * cw[0:1] + xm1 * cw[1:2] + x * cw[2:3] + xp1 * cw[3:4]
        gates = jnp.dot(xc.astype(BF16), wg_ref[0, s], preferred_element_type=F32) + bg_ref[0, s]
        r = _sigmoid(gates[:, :LANES])
        ig = _sigmoid(gates[:, LANES:])
        log_a = (-LRU_C * jax.nn.softplus(-lam_ref[0, s])) * r
        a = jnp.exp(log_a)
        a_ref[s] = a
        u_ref[s] = jnp.sqrt(1.0 - a * a) * (ig * xc)

    base = chunk * rows

    def fwd_step(t, hs):
        off = pl.multiple_of(t * BATCH_GROUP, BATCH_GROUP)
        new = []
        for s in range(SLABS):
            h = a_ref[s, pl.ds(off, BATCH_GROUP), :] * hs[s] + u_ref[s, pl.ds(off, BATCH_GROUP), :]
            hf_ref[s, pl.ds(pl.multiple_of(base + off, BATCH_GROUP), BATCH_GROUP), :] = h
            new.append(h)
        return tuple(new)

    def bwd_step(k, hs):
        off = pl.multiple_of((tc - 1 - k) * BATCH_GROUP, BATCH_GROUP)
        new = []
        for s in range(SLABS):
            h = a_ref[s, pl.ds(off, BATCH_GROUP), :] * hs[s] + u_ref[s, pl.ds(off, BATCH_GROUP), :]
            rl_ref[s, pl.ds(off, BATCH_GROUP), :] = (
                h + hf_ref[s, pl.ds(pl.multiple_of(base + off, BATCH_GROUP), BATCH_GROUP), :])
            new.append(h)
        return tuple(new)

    init = tuple(carry_ref[s] for s in range(SLABS))

    @pl.when(jnp.logical_not(backward))
    def _():
        hs = lax.fori_loop(0, tc, fwd_step, init, unroll=8)
        for s in range(SLABS):
            carry_ref[s] = hs[s]

    @pl.when(backward)
    def _():
        hs = lax.fori_loop(0, tc, bwd_step, init, unroll=8)
        for s in range(SLABS):
            carry_ref[s] = hs[s]
        for s in range(SLABS):
            for b in range(BATCH_GROUP):
                hb = rl_ref[s, pl.ds(b, tc, stride=BATCH_GROUP), :]
                sl = slice(s * LANES, (s + 1) * LANES)
                o_ref[b, :, sl] = (hb * gate_ref[b, :, sl].astype(F32)).astype(BF16)


def _recurrent(xr, gate, conv_w, conv_b, wg, bg, lam, *, tc):
    nbg, _, rows_total, _ = xr.shape
    S = rows_total // BATCH_GROUP
    B = nbg * BATCH_GROUP
    nchunk = S // tc
    rows = tc * BATCH_GROUP
    grid = (nbg, LRU_BLOCKS // SLABS, 2 * nchunk)

    def chunk_of(j):
        return jnp.where(j >= nchunk, 2 * nchunk - 1 - j, j)

    def out_chunk(j):
        return jnp.where(j >= nchunk, 2 * nchunk - 1 - j, nchunk - 1)

    main = pl.BlockSpec((1, SLABS, rows, LANES), lambda g, c, j: (g, c, chunk_of(j), 0))
    prev = pl.BlockSpec(
        (1, SLABS, 2 * BATCH_GROUP, LANES),
        lambda g, c, j: (g, c, jnp.maximum(chunk_of(j) * (tc // 2) - 1, 0), 0))
    nxt = pl.BlockSpec(
        (1, SLABS, BATCH_GROUP, LANES),
        lambda g, c, j: (g, c, jnp.minimum((chunk_of(j) + 1) * tc, S - 1), 0))
    in_specs = [
        main, prev, nxt,
        pl.BlockSpec((SLABS, 4, LANES), lambda g, c, j: (c, 0, 0)),
        pl.BlockSpec((SLABS, 1, LANES), lambda g, c, j: (c, 0, 0)),
        pl.BlockSpec((1, SLABS, LANES, 2 * LANES), lambda g, c, j: (j // nchunk, c, 0, 0)),
        pl.BlockSpec((1, SLABS, 1, 2 * LANES), lambda g, c, j: (j // nchunk, c, 0, 0)),
        pl.BlockSpec((1, SLABS, 1, LANES), lambda g, c, j: (j // nchunk, c, 0, 0)),
        pl.BlockSpec((BATCH_GROUP, tc, SLABS * LANES), lambda g, c, j: (g, out_chunk(j), c)),
    ]
    return pl.pallas_call(
        functools.partial(_rec_kernel, tc=tc, nchunk=nchunk),
        grid=grid,
        in_specs=in_specs,
        out_specs=pl.BlockSpec((BATCH_GROUP, tc, SLABS * LANES),
                               lambda g, c, j: (g, out_chunk(j), c)),
        out_shape=jax.ShapeDtypeStruct((B, S, LRU_WIDTH), BF16),
        scratch_shapes=[
            pltpu.VMEM((SLABS, rows_total, LANES), F32),
            pltpu.VMEM((SLABS, rows, LANES), F32),
            pltpu.VMEM((SLABS, rows, LANES), F32),
            pltpu.VMEM((SLABS, rows, LANES), F32),
            pltpu.VMEM((SLABS, BATCH_GROUP, LANES), F32),
        ],
        compiler_params=pltpu.CompilerParams(
            dimension_semantics=("parallel", "parallel", "arbitrary"),
            vmem_limit_bytes=VMEM_LIMIT),
        name="recurrent",
    )(xr, xr, xr, conv_w, conv_b, wg, bg, lam, gate)


def _merge_kernel(x_ref, at_ref, rc_ref, ga_ref, gr_ref, pa_ref, pr_ref, wo_ref, n1_ref, n2_ref,
                  x1_ref, hn_ref):
    ya = jnp.dot(at_ref[...], pa_ref[...], preferred_element_type=F32)
    yr = jnp.dot(rc_ref[...], pr_ref[...], preferred_element_type=F32)
    merged = ga_ref[...].astype(F32) * ya + gr_ref[...].astype(F32) * yr
    mix = jnp.dot(merged.astype(BF16), wo_ref[...], preferred_element_type=F32)
    x1 = x_ref[...] + _rms(mix, n1_ref[...])
    x1_ref[...] = x1
    hn_ref[...] = _rms(x1, n2_ref[...]).astype(BF16)


def _merge(x, attn, rec, ga, gr, pa, pr, wo, n1, n2, *, tm):
    T = x.shape[0]

    def tok(width):
        return pl.BlockSpec((tm, width), lambda i: (i, 0))

    return pl.pallas_call(
        _merge_kernel,
        grid=(T // tm,),
        in_specs=[tok(D_MODEL), tok(Q_WIDTH), tok(LRU_WIDTH), tok(D_MODEL), tok(D_MODEL),
                  _const_spec((Q_WIDTH, D_MODEL)), _const_spec((LRU_WIDTH, D_MODEL)),
                  _const_spec((D_MODEL, D_MODEL)), _const_spec((1, D_MODEL)),
                  _const_spec((1, D_MODEL))],
        out_specs=(tok(D_MODEL), tok(D_MODEL)),
        out_shape=(jax.ShapeDtypeStruct((T, D_MODEL), F32),
                   jax.ShapeDtypeStruct((T, D_MODEL), BF16)),
        compiler_params=pltpu.CompilerParams(
            dimension_semantics=("parallel",), vmem_limit_bytes=VMEM_LIMIT),
        name="merge",
    )(x, attn, rec, ga, gr, pa, pr, wo, n1, n2)


FF_CHUNK = 256


def _ffn_kernel(x1_ref, hn_ref, wi_ref, wo_ref, n_ref, o_ref, act_ref):
    h = hn_ref[...]
    for c in range(D_FF // FF_CHUNK):
        lo = c * FF_CHUNK
        g = jnp.dot(h, wi_ref[:, lo:lo + FF_CHUNK], preferred_element_type=F32)
        u = jnp.dot(h, wi_ref[:, D_FF + lo:D_FF + lo + FF_CHUNK], preferred_element_type=F32)
        act_ref[:, lo:lo + FF_CHUNK] = (g * _sigmoid(g) * u).astype(BF16)
    f = jnp.dot(act_ref[...], wo_ref[...], preferred_element_type=F32)
    o_ref[...] = x1_ref[...] + _rms(f, n_ref[...])


def _ffn(x1, hn, wi, wo, nw, *, tm):
    T = x1.shape[0]

    def tok(width):
        return pl.BlockSpec((tm, width), lambda i: (i, 0))

    return pl.pallas_call(
        _ffn_kernel,
        grid=(T // tm,),
        in_specs=[tok(D_MODEL), tok(D_MODEL), _const_spec((D_MODEL, 2 * D_FF)),
                  _const_spec((D_FF, D_MODEL)), _const_spec((1, D_MODEL))],
        out_specs=tok(D_MODEL),
        out_shape=jax.ShapeDtypeStruct((T, D_MODEL), F32),
        scratch_shapes=[pltpu.VMEM((tm, D_FF), BF16)],
        compiler_params=pltpu.CompilerParams(
            dimension_semantics=("parallel",), vmem_limit_bytes=VMEM_LIMIT),
        name="ffn",
    )(x1, hn, wi, wo, nw)


def _rope_tables(S):
    half = ROT_DIM // 2
    inv = ROPE_THETA ** (-jnp.arange(half, dtype=F32) / half)
    ang = jnp.arange(S).astype(F32)[:, None] * inv[None, :]
    cos, sin = jnp.cos(ang), jnp.sin(ang)
    zeros = jnp.zeros((S, HEAD_DIM - ROT_DIM), F32)
    zh = jnp.zeros((S, half), F32)
    rope_c = jnp.concatenate([cos, cos, jnp.ones_like(zeros)], axis=1)
    rope_s1 = jnp.concatenate([zh, sin, zeros], axis=1)
    rope_s2 = jnp.concatenate([-sin, zh, zeros], axis=1)
    return rope_c, rope_s1, rope_s2


def _pick(S, prefer):
    return prefer if S % prefer == 0 else S


def _layer(x, p):
    B, S, _ = x.shape
    tt = _pick(S, 64)
    q, k, v, xr, rg, ga, gr = _in_proj(x, p["norm_mix_pre"], p["w_in"], *_rope_tables(S), tt=tt)
    attn = _attention(q, k, v, p["attn_sink"], tq=_pick(S, 512))
    rec = _recurrent(xr, rg, p["conv_w"], p["conv_b"], p["wg"], p["bg"], p["lam"], tc=_pick(S, 128))
    T = B * S
    tm = _pick(T, 512)
    x1, hn = _merge(x.reshape(T, D_MODEL), attn.reshape(T, Q_WIDTH), rec.reshape(T, LRU_WIDTH),
                    ga.reshape(T, D_MODEL), gr.reshape(T, D_MODEL),
                    p["w_attn_proj"], p["w_rec_proj"], p["w_out"],
                    p["norm_mix_post"], p["norm_ffn_pre"], tm=tm)
    y = _ffn(x1, hn, p["w_ffn_in"], p["w_ffn_out"], p["norm_ffn_post"], tm=tm)
    return y.reshape(B, S, D_MODEL)


def _prep(norm_mix_pre, w_in, attn_sink, conv_w, conv_b, lru_w_a, lru_b_a, lru_w_i, lru_b_i,
          lru_lambda, w_attn_proj, w_rec_proj, w_out, norm_mix_post, norm_ffn_pre, w_ffn_in,
          w_ffn_out, norm_ffn_post):
    n_dirs = lru_w_a.shape[0]
    wg = jnp.concatenate([lru_w_a, lru_w_i], axis=-1).astype(BF16)
    bg = jnp.concatenate(
        [lru_b_a.reshape(n_dirs, LRU_BLOCKS, 1, LANES), lru_b_i.reshape(n_dirs, LRU_BLOCKS, 1, LANES)],
        axis=-1)
    return dict(
        norm_mix_pre=norm_mix_pre.reshape(1, D_MODEL),
        w_in=w_in.astype(BF16),
        attn_sink=attn_sink,
        conv_w=conv_w.reshape(4, LRU_BLOCKS, LANES).transpose(1, 0, 2),
        conv_b=conv_b.reshape(LRU_BLOCKS, 1, LANES),
        wg=wg, bg=bg,
        lam=lru_lambda.reshape(n_dirs, LRU_BLOCKS, 1, LANES),
        w_attn_proj=w_attn_proj.astype(BF16),
        w_rec_proj=w_rec_proj.astype(BF16),
        w_out=w_out.astype(BF16),
        norm_mix_post=norm_mix_post.reshape(1, D_MODEL),
        norm_ffn_pre=norm_ffn_pre.reshape(1, D_MODEL),
        w_ffn_in=w_ffn_in.astype(BF16),
        w_ffn_out=w_ffn_out.astype(BF16),
        norm_ffn_post=norm_ffn_post.reshape(1, D_MODEL),
    )


def kernel(x_prompt, x_sample, norm_mix_pre, w_in, attn_sink, conv_w, conv_b, lru_w_a, lru_b_a,
           lru_w_i, lru_b_i, lru_lambda, w_attn_proj, w_rec_proj, w_out, norm_mix_post,
           norm_ffn_pre, w_ffn_in, w_ffn_out, norm_ffn_post):
    params = (norm_mix_pre, w_in, attn_sink, conv_w, conv_b, lru_w_a, lru_b_a, lru_w_i, lru_b_i,
              lru_lambda, w_attn_proj, w_rec_proj, w_out, norm_mix_post, norm_ffn_pre, w_ffn_in,
              w_ffn_out, norm_ffn_post)
    y_prompt, y_sample = x_prompt, x_sample
    for l in range(norm_mix_pre.shape[0]):
        p = _prep(*(a[l] for a in params))
        y_prompt = _layer(y_prompt, p)
        y_sample = _layer(y_sample, p)
    return (y_prompt, y_sample)
```

```python
import functools
import math

import jax
import jax.numpy as jnp
from jax import lax
from jax.experimental import pallas as pl
from jax.experimental.pallas import tpu as pltpu

D_MODEL = 1024
HEAD_DIM = 128
N_Q_HEADS = 8
N_KV_HEADS = 2
GROUP = N_Q_HEADS // N_KV_HEADS
Q_WIDTH = N_Q_HEADS * HEAD_DIM
KV_WIDTH = N_KV_HEADS * HEAD_DIM
WINDOW = 128
ROT_DIM = HEAD_DIM // 4
ROPE_THETA = 500000.0
NEG_BIG = -1e30
LRU_WIDTH = 1280
LRU_BLOCKS = 10
LRU_BLOCK_DIM = 128
LRU_C = 8.0
D_FF = 2816
EPS = 1e-6
IN_WIDTH = Q_WIDTH + 2 * KV_WIDTH + 2 * LRU_WIDTH + 2 * D_MODEL
O_Q = 0
O_K = O_Q + Q_WIDTH
O_V = O_K + KV_WIDTH
O_RX = O_V + KV_WIDTH
O_RG = O_RX + LRU_WIDTH
O_GA = O_RG + LRU_WIDTH
O_GR = O_GA + D_MODEL

LANES = 128
SUBLANES = 8
BATCH_GROUP = SUBLANES
VMEM_LIMIT = 56 * 1024 * 1024

BF16 = jnp.bfloat16
F32 = jnp.float32
LOG2E = 1.4426950408889634


def _const_spec(shape):
    nd = len(shape)
    return pl.BlockSpec(shape, lambda *_: (0,) * nd, pipeline_mode=pl.Buffered(1))


def _sigmoid(x):
    return 0.5 * jnp.tanh(0.5 * x) + 0.5


def _rms(x, w):
    ms = jnp.mean(x * x, axis=-1, keepdims=True)
    return x * lax.rsqrt(ms + EPS) * w


def _in_proj_kernel(x_ref, nw_ref, w_ref, rc_ref, rs1_ref, rs2_ref,
                    q_ref, k_ref, v_ref, xr_ref, rg_ref, ga_ref, gr_ref, *, tt):
    rows = BATCH_GROUP * tt
    x = x_ref[...].reshape(rows, D_MODEL)
    h = _rms(x, nw_ref[...]).astype(BF16)

    def proj(lo, width):
        return jnp.dot(h, w_ref[:, lo:lo + width], preferred_element_type=F32)

    rc = rc_ref[...][None]
    rs1 = rs1_ref[...][None]
    rs2 = rs2_ref[...][None]

    def rope(t):
        up = pltpu.roll(t, ROT_DIM // 2, axis=1).reshape(BATCH_GROUP, tt, HEAD_DIM)
        dn = pltpu.roll(t, HEAD_DIM - ROT_DIM // 2, axis=1).reshape(BATCH_GROUP, tt, HEAD_DIM)
        return t.reshape(BATCH_GROUP, tt, HEAD_DIM) * rc + up * rs1 + dn * rs2

    scale = LOG2E / math.sqrt(HEAD_DIM)
    zq = proj(O_Q, Q_WIDTH)
    for hh in range(N_Q_HEADS):
        sl = slice(hh * HEAD_DIM, (hh + 1) * HEAD_DIM)
        q_ref[:, :, sl] = (rope(zq[:, sl]) * scale).astype(BF16)
    zk = proj(O_K, KV_WIDTH)
    for hh in range(N_KV_HEADS):
        sl = slice(hh * HEAD_DIM, (hh + 1) * HEAD_DIM)
        k_ref[:, :, sl] = rope(zk[:, sl]).astype(BF16)
    v_ref[...] = proj(O_V, KV_WIDTH).astype(BF16).reshape(BATCH_GROUP, tt, KV_WIDTH)

    zx = proj(O_RX, LRU_WIDTH)
    for n in range(LRU_BLOCKS):
        for b in range(BATCH_GROUP):
            xr_ref[0, n, pl.ds(b, tt, stride=BATCH_GROUP), :] = (
                zx[b * tt:(b + 1) * tt, n * LANES:(n + 1) * LANES])

    rg_ref[...] = jax.nn.gelu(proj(O_RG, LRU_WIDTH)).astype(BF16).reshape(BATCH_GROUP, tt, LRU_WIDTH)
    ga_ref[...] = _sigmoid(proj(O_GA, D_MODEL)).astype(BF16).reshape(BATCH_GROUP, tt, D_MODEL)
    gr_ref[...] = _sigmoid(proj(O_GR, D_MODEL)).astype(BF16).reshape(BATCH_GROUP, tt, D_MODEL)


def _in_proj(x, norm_w, w_in, rope_c, rope_s1, rope_s2, *, tt):
    B, S, _ = x.shape
    nbg = B // BATCH_GROUP
    grid = (nbg, S // tt)

    def bm(width):
        return pl.BlockSpec((BATCH_GROUP, tt, width), lambda g, i: (g, i, 0))

    rope_spec = pl.BlockSpec((tt, HEAD_DIM), lambda g, i: (i, 0))
    out_shape = (
        jax.ShapeDtypeStruct((B, S, Q_WIDTH), BF16),
        jax.ShapeDtypeStruct((B, S, KV_WIDTH), BF16),
        jax.ShapeDtypeStruct((B, S, KV_WIDTH), BF16),
        jax.ShapeDtypeStruct((nbg, LRU_BLOCKS, S * BATCH_GROUP, LANES), F32),
        jax.ShapeDtypeStruct((B, S, LRU_WIDTH), BF16),
        jax.ShapeDtypeStruct((B, S, D_MODEL), BF16),
        jax.ShapeDtypeStruct((B, S, D_MODEL), BF16),
    )
    out_specs = (
        bm(Q_WIDTH), bm(KV_WIDTH), bm(KV_WIDTH),
        pl.BlockSpec((1, LRU_BLOCKS, tt * BATCH_GROUP, LANES), lambda g, i: (g, 0, i, 0)),
        bm(LRU_WIDTH), bm(D_MODEL), bm(D_MODEL),
    )
    return pl.pallas_call(
        functools.partial(_in_proj_kernel, tt=tt),
        grid=grid,
        in_specs=[bm(D_MODEL), _const_spec((1, D_MODEL)), _const_spec((D_MODEL, IN_WIDTH)),
                  rope_spec, rope_spec, rope_spec],
        out_specs=out_specs,
        out_shape=out_shape,
        compiler_params=pltpu.CompilerParams(
            dimension_semantics=("parallel", "parallel"), vmem_limit_bytes=VMEM_LIMIT),
        name="in_proj",
    )(x, norm_w, w_in, rope_c, rope_s1, rope_s2)


def _attn_kernel(sink_ref, q_ref, kp_ref, km_ref, kn_ref, vp_ref, vm_ref, vn_ref, o_ref, *, tq, nblk):
    i = pl.program_id(1)
    nsub = tq // WINDOW
    kcat = jnp.concatenate([kp_ref[0], km_ref[0], kn_ref[0]], axis=0)
    vcat = jnp.concatenate([vp_ref[0], vm_ref[0], vn_ref[0]], axis=0)
    rows = GROUP * WINDOW
    r = lax.broadcasted_iota(jnp.int32, (WINDOW, WINDOW), 0)
    c = lax.broadcasted_iota(jnp.int32, (WINDOW, WINDOW), 1)
    tri_prev = jnp.where(c >= r, 0.0, NEG_BIG).astype(F32)
    tri_next = jnp.where(c <= r, 0.0, NEG_BIG).astype(F32)
    head_of_row = lax.broadcasted_iota(jnp.int32, (rows, 1), 0) // WINDOW

    def banded(x, bias):
        return (x.reshape(GROUP, WINDOW, WINDOW) + bias[None]).reshape(rows, WINDOW)

    for j in range(nsub):
        blk = i * nsub + j
        bias_prev = jnp.where(blk == 0, NEG_BIG, tri_prev)
        bias_next = jnp.where(blk == nblk - 1, NEG_BIG, tri_next)
        for g in range(N_KV_HEADS):
            ks = kcat[j * WINDOW:(j + 3) * WINDOW, g * HEAD_DIM:(g + 1) * HEAD_DIM]
            vs = vcat[j * WINDOW:(j + 3) * WINDOW, g * HEAD_DIM:(g + 1) * HEAD_DIM]
            qs = jnp.concatenate(
                [q_ref[0, j * WINDOW:(j + 1) * WINDOW,
                       (g * GROUP + hh) * HEAD_DIM:(g * GROUP + hh + 1) * HEAD_DIM]
                 for hh in range(GROUP)], axis=0)
            s = lax.dot_general(qs, ks, (((1,), (1,)), ((), ())), preferred_element_type=F32)
            sp = banded(s[:, :WINDOW], bias_prev)
            sc = s[:, WINDOW:2 * WINDOW]
            sn = banded(s[:, 2 * WINDOW:], bias_next)
            sink = jnp.zeros((rows, 1), F32)
            for hh in range(GROUP):
                sink = jnp.where(head_of_row == hh, sink_ref[g * GROUP + hh] * LOG2E, sink)
            m = jnp.max(jnp.maximum(jnp.maximum(sp, sc), sn), axis=-1, keepdims=True)
            m = jnp.maximum(m, sink)
            pp, pc, pn = jnp.exp2(sp - m), jnp.exp2(sc - m), jnp.exp2(sn - m)
            denom = jnp.sum(pp + pc + pn, axis=-1, keepdims=True) + jnp.exp2(sink - m)
            p = jnp.concatenate([pp.astype(BF16), pc.astype(BF16), pn.astype(BF16)], axis=1)
            o = jnp.dot(p, vs, preferred_element_type=F32) / denom
            for hh in range(GROUP):
                col = (g * GROUP + hh) * HEAD_DIM
                o_ref[0, j * WINDOW:(j + 1) * WINDOW, col:col + HEAD_DIM] = (
                    o[hh * WINDOW:(hh + 1) * WINDOW].astype(BF16))


def _attention(q, k, v, sink, *, tq):
    B, S, _ = q.shape
    nblk = S // WINDOW
    nsub = tq // WINDOW
    grid = (B, S // tq)
    main = pl.BlockSpec((1, tq, KV_WIDTH), lambda b, i: (b, i, 0))
    prev = pl.BlockSpec((1, WINDOW, KV_WIDTH), lambda b, i: (b, jnp.maximum(i * nsub - 1, 0), 0))
    nxt = pl.BlockSpec((1, WINDOW, KV_WIDTH),
                       lambda b, i: (b, jnp.minimum((i + 1) * nsub, nblk - 1), 0))
    qspec = pl.BlockSpec((1, tq, Q_WIDTH), lambda b, i: (b, i, 0))
    return pl.pallas_call(
        functools.partial(_attn_kernel, tq=tq, nblk=nblk),
        grid=grid,
        in_specs=[pl.BlockSpec(memory_space=pltpu.SMEM), qspec, prev, main, nxt, prev, main, nxt],
        out_specs=qspec,
        out_shape=jax.ShapeDtypeStruct((B, S, Q_WIDTH), BF16),
        compiler_params=pltpu.CompilerParams(
            dimension_semantics=("parallel", "parallel"), vmem_limit_bytes=VMEM_LIMIT),
        name="attention",
    )(sink, q, k, k, k, v, v, v)


SLABS = 2


CONV_TAPS = 4


def _rec_kernel(xm_ref, xp_ref, xn_ref, cw_ref, cb_ref, wg_ref, bg_ref, lam_ref, gate_ref,
                o_ref, hf_ref, rl_ref, carry_ref, *, tc, nchunk):
    j = pl.program_id(2)
    backward = j >= nchunk
    chunk = jnp.where(backward, 2 * nchunk - 1 - j, j)
    rows = tc * BATCH_GROUP

    @pl.when((j == 0) | (j == nchunk))
    def _():
        carry_ref[...] = jnp.zeros_like(carry_ref)

    has_prev = (chunk > 0).astype(F32)
    has_next = (chunk < nchunk - 1).astype(F32)
    base = chunk * rows

    def coeffs(s):
        xe = jnp.concatenate(
            [xp_ref[0, s] * has_prev, xm_ref[0, s], xn_ref[0, s] * has_next], axis=0)
        cw = 0.5 * cw_ref[s]
        xh = 0.5 * cb_ref[s] + xe[0:rows] * cw[0:1]
        for k in range(1, CONV_TAPS):
            xh = xh + xe[k * BATCH_GROUP:k * BATCH_GROUP + rows] * cw[k:k + 1]
        g = jnp.dot(xh.astype(BF16), wg_ref[0, s], preferred_element_type=F32) + 0.5 * bg_ref[0, s]
        tr = jnp.tanh(g[:, 0:LANES])
        ti = jnp.tanh(g[:, LANES:])
        c = (-0.5 * LRU_C * LOG2E) * jax.nn.softplus(-lam_ref[0, s])
        a = jnp.exp2(c * tr + c)
        y = 1.0 - a * a
        root = jnp.where(y > 0.0, y * lax.rsqrt(y), 0.0)
        u = root * ((ti + 1.0) * xh)
        return a, u

    def tile(v, t):
        return v[t * BATCH_GROUP:(t + 1) * BATCH_GROUP]

    def state_rows(t):
        return pl.ds(pl.multiple_of(base + t * BATCH_GROUP, BATCH_GROUP), BATCH_GROUP)

    @pl.when(jnp.logical_not(backward))
    def _():
        for s in range(SLABS):
            a, u = coeffs(s)
            h = carry_ref[s]
            for t in range(tc):
                h = tile(a, t) * h + tile(u, t)
                hf_ref[s, state_rows(t), :] = h
            carry_ref[s] = h

    @pl.when(backward)
    def _():
        for s in range(SLABS):
            a, u = coeffs(s)
            h = carry_ref[s]
            for t in reversed(range(tc)):
                h = tile(a, t) * h + tile(u, t)
                rl_ref[s, t * BATCH_GROUP:(t + 1) * BATCH_GROUP, :] = h + hf_ref[s, state_rows(t), :]
            carry_ref[s] = h
        for s in range(SLABS):
            for b in range(BATCH_GROUP):
                hb = rl_ref[s, pl.ds(b, tc, stride=BATCH_GROUP), :]
                sl = slice(s * LANES, (s + 1) * LANES)
                o_ref[b, :, sl] = (hb * gate_ref[b, :, sl].astype(F32)).astype(BF16)


def _recurrent(xr, gate, conv_w, conv_b, wg, bg, lam, *, tc):
    nbg, _, rows_total, _ = xr.shape
    S = rows_total // BATCH_GROUP
    B = nbg * BATCH_GROUP
    nchunk = S // tc
    rows = tc * BATCH_GROUP
    grid = (nbg, LRU_BLOCKS // SLABS, 2 * nchunk)

    def chunk_of(j):
        return jnp.where(j >= nchunk, 2 * nchunk - 1 - j, j)

    def out_chunk(j):
        return jnp.where(j >= nchunk, 2 * nchunk - 1 - j, nchunk - 1)

    main = pl.BlockSpec((1, SLABS, rows, LANES), lambda g, c, j: (g, c, chunk_of(j), 0))
    prev = pl.BlockSpec(
        (1, SLABS, 2 * BATCH_GROUP, LANES),
        lambda g, c, j: (g, c, jnp.maximum(chunk_of(j) * (tc // 2) - 1, 0), 0))
    nxt = pl.BlockSpec(
        (1, SLABS, BATCH_GROUP, LANES),
        lambda g, c, j: (g, c, jnp.minimum((chunk_of(j) + 1) * tc, S - 1), 0))
    in_specs = [
        main, prev, nxt,
        pl.BlockSpec((SLABS, CONV_TAPS, LANES), lambda g, c, j: (c, 0, 0)),
        pl.BlockSpec((SLABS, 1, LANES), lambda g, c, j: (c, 0, 0)),
        pl.BlockSpec((1, SLABS, LANES, 2 * LANES), lambda g, c, j: (j // nchunk, c, 0, 0)),
        pl.BlockSpec((1, SLABS, 1, 2 * LANES), lambda g, c, j: (j // nchunk, c, 0, 0)),
        pl.BlockSpec((1, SLABS, 1, LANES), lambda g, c, j: (j // nchunk, c, 0, 0)),
        pl.BlockSpec((BATCH_GROUP, tc, SLABS * LANES), lambda g, c, j: (g, out_chunk(j), c)),
    ]
    return pl.pallas_call(
        functools.partial(_rec_kernel, tc=tc, nchunk=nchunk),
        grid=grid,
        in_specs=in_specs,
        out_specs=pl.BlockSpec((BATCH_GROUP, tc, SLABS * LANES),
                               lambda g, c, j: (g, out_chunk(j), c)),
        out_shape=jax.ShapeDtypeStruct((B, S, LRU_WIDTH), BF16),
        scratch_shapes=[
            pltpu.VMEM((SLABS, rows_total, LANES), F32),
            pltpu.VMEM((SLABS, rows, LANES), F32),
            pltpu.VMEM((SLABS, BATCH_GROUP, LANES), F32),
        ],
        compiler_params=pltpu.CompilerParams(
            dimension_semantics=("parallel", "parallel", "arbitrary"),
            vmem_limit_bytes=VMEM_LIMIT),
        name="recurrent",
    )(xr, xr, xr, conv_w, conv_b, wg, bg, lam, gate)


def _merge_kernel(x_ref, at_ref, rc_ref, ga_ref, gr_ref, pa_ref, pr_ref, wo_ref, n1_ref, n2_ref,
                  x1_ref, hn_ref):
    ya = jnp.dot(at_ref[...], pa_ref[...], preferred_element_type=F32)
    yr = jnp.dot(rc_ref[...], pr_ref[...], preferred_element_type=F32)
    merged = ga_ref[...].astype(F32) * ya + gr_ref[...].astype(F32) * yr
    mix = jnp.dot(merged.astype(BF16), wo_ref[...], preferred_element_type=F32)
    x1 = x_ref[...] + _rms(mix, n1_ref[...])
    x1_ref[...] = x1
    hn_ref[...] = _rms(x1, n2_ref[...]).astype(BF16)


def _merge(x, attn, rec, ga, gr, pa, pr, wo, n1, n2, *, tm):
    T = x.shape[0]

    def tok(width):
        return pl.BlockSpec((tm, width), lambda i: (i, 0))

    return pl.pallas_call(
        _merge_kernel,
        grid=(T // tm,),
        in_specs=[tok(D_MODEL), tok(Q_WIDTH), tok(LRU_WIDTH), tok(D_MODEL), tok(D_MODEL),
                  _const_spec((Q_WIDTH, D_MODEL)), _const_spec((LRU_WIDTH, D_MODEL)),
                  _const_spec((D_MODEL, D_MODEL)), _const_spec((1, D_MODEL)),
                  _const_spec((1, D_MODEL))],
        out_specs=(tok(D_MODEL), tok(D_MODEL)),
        out_shape=(jax.ShapeDtypeStruct((T, D_MODEL), F32),
                   jax.ShapeDtypeStruct((T, D_MODEL), BF16)),
        compiler_params=pltpu.CompilerParams(
            dimension_semantics=("parallel",), vmem_limit_bytes=VMEM_LIMIT),
        name="merge",
    )(x, attn, rec, ga, gr, pa, pr, wo, n1, n2)


FF_CHUNK = 256


def _ffn_kernel(x1_ref, hn_ref, wi_ref, wo_ref, n_ref, o_ref, act_ref):
    h = hn_ref[...]
    for c in range(D_FF // FF_CHUNK):
        lo = c * FF_CHUNK
        g = jnp.dot(h, wi_ref[:, lo:lo + FF_CHUNK], preferred_element_type=F32)
        u = jnp.dot(h, wi_ref[:, D_FF + lo:D_FF + lo + FF_CHUNK], preferred_element_type=F32)
        act_ref[:, lo:lo + FF_CHUNK] = (g * _sigmoid(g) * u).astype(BF16)
    f = jnp.dot(act_ref[...], wo_ref[...], preferred_element_type=F32)
    o_ref[...] = x1_ref[...] + _rms(f, n_ref[...])


def _ffn(x1, hn, wi, wo, nw, *, tm):
    T = x1.shape[0]

    def tok(width):
        return pl.BlockSpec((tm, width), lambda i: (i, 0))

    return pl.pallas_call(
        _ffn_kernel,
        grid=(T // tm,),
        in_specs=[tok(D_MODEL), tok(D_MODEL), _const_spec((D_MODEL, 2 * D_FF)),
                  _const_spec((D_FF, D_MODEL)), _const_spec((1, D_MODEL))],
        out_specs=tok(D_MODEL),
        out_shape=jax.ShapeDtypeStruct((T, D_MODEL), F32),
        scratch_shapes=[pltpu.VMEM((tm, D_FF), BF16)],
        compiler_params=pltpu.CompilerParams(
            dimension_semantics=("parallel",), vmem_limit_bytes=VMEM_LIMIT),
        name="ffn",
    )(x1, hn, wi, wo, nw)


def _rope_tables(S):
    half = ROT_DIM // 2
    inv = ROPE_THETA ** (-jnp.arange(half, dtype=F32) / half)
    ang = jnp.arange(S).astype(F32)[:, None] * inv[None, :]
    cos, sin = jnp.cos(ang), jnp.sin(ang)
    zeros = jnp.zeros((S, HEAD_DIM - ROT_DIM), F32)
    zh = jnp.zeros((S, half), F32)
    rope_c = jnp.concatenate([cos, cos, jnp.ones_like(zeros)], axis=1)
    rope_s1 = jnp.concatenate([zh, sin, zeros], axis=1)
    rope_s2 = jnp.concatenate([-sin, zh, zeros], axis=1)
    return rope_c, rope_s1, rope_s2


def _pick(S, prefer):
    return prefer if S % prefer == 0 else S


def _layer(x, p):
    B, S, _ = x.shape
    tt = _pick(S, 64)
    q, k, v, xr, rg, ga, gr = _in_proj(x, p["norm_mix_pre"], p["w_in"], *_rope_tables(S), tt=tt)
    attn = _attention(q, k, v, p["attn_sink"], tq=_pick(S, 512))
    rec = _recurrent(xr, rg, p["conv_w"], p["conv_b"], p["wg"], p["bg"], p["lam"], tc=_pick(S, 128))
    T = B * S
    tm = _pick(T, 512)
    x1, hn = _merge(x.reshape(T, D_MODEL), attn.reshape(T, Q_WIDTH), rec.reshape(T, LRU_WIDTH),
                    ga.reshape(T, D_MODEL), gr.reshape(T, D_MODEL),
                    p["w_attn_proj"], p["w_rec_proj"], p["w_out"],
                    p["norm_mix_post"], p["norm_ffn_pre"], tm=tm)
    y = _ffn(x1, hn, p["w_ffn_in"], p["w_ffn_out"], p["norm_ffn_post"], tm=tm)
    return y.reshape(B, S, D_MODEL)


def _prep(norm_mix_pre, w_in, attn_sink, conv_w, conv_b, lru_w_a, lru_b_a, lru_w_i, lru_b_i,
          lru_lambda, w_attn_proj, w_rec_proj, w_out, norm_mix_post, norm_ffn_pre, w_ffn_in,
          w_ffn_out, norm_ffn_post):
    n_dirs = lru_w_a.shape[0]
    wg = jnp.concatenate([lru_w_a, lru_w_i], axis=-1).astype(BF16)
    bg = jnp.concatenate(
        [lru_b_a.reshape(n_dirs, LRU_BLOCKS, 1, LANES), lru_b_i.reshape(n_dirs, LRU_BLOCKS, 1, LANES)],
        axis=-1)
    return dict(
        norm_mix_pre=norm_mix_pre.reshape(1, D_MODEL),
        w_in=w_in.astype(BF16),
        attn_sink=attn_sink,
        conv_w=conv_w.reshape(CONV_TAPS, LRU_BLOCKS, LANES).transpose(1, 0, 2),
        conv_b=conv_b.reshape(LRU_BLOCKS, 1, LANES),
        wg=wg, bg=bg,
        lam=lru_lambda.reshape(n_dirs, LRU_BLOCKS, 1, LANES),
        w_attn_proj=w_attn_proj.astype(BF16),
        w_rec_proj=w_rec_proj.astype(BF16),
        w_out=w_out.astype(BF16),
        norm_mix_post=norm_mix_post.reshape(1, D_MODEL),
        norm_ffn_pre=norm_ffn_pre.reshape(1, D_MODEL),
        w_ffn_in=w_ffn_in.astype(BF16),
        w_ffn_out=w_ffn_out.astype(BF16),
        norm_ffn_post=norm_ffn_post.reshape(1, D_MODEL),
    )


def kernel(x_prompt, x_sample, norm_mix_pre, w_in, attn_sink, conv_w, conv_b, lru_w_a, lru_b_a,
           lru_w_i, lru_b_i, lru_lambda, w_attn_proj, w_rec_proj, w_out, norm_mix_post,
           norm_ffn_pre, w_ffn_in, w_ffn_out, norm_ffn_post):
    params = (norm_mix_pre, w_in, attn_sink, conv_w, conv_b, lru_w_a, lru_b_a, lru_w_i, lru_b_i,
              lru_lambda, w_attn_proj, w_rec_proj, w_out, norm_mix_post, norm_ffn_pre, w_ffn_in,
              w_ffn_out, norm_ffn_post)
    y_prompt, y_sample = x_prompt, x_sample
    for l in range(norm_mix_pre.shape[0]):
        p = _prep(*(a[l] for a in params))
        y_prompt = _layer(y_prompt, p)
        y_sample = _layer(y_sample, p)
    return (y_prompt, y_sample)
```

```python
import functools
import math

import jax
import jax.numpy as jnp
from jax import lax
from jax.experimental import pallas as pl
from jax.experimental.pallas import tpu as pltpu

D_MODEL = 1024
HEAD_DIM = 128
N_Q_HEADS = 8
N_KV_HEADS = 2
GROUP = N_Q_HEADS // N_KV_HEADS
Q_WIDTH = N_Q_HEADS * HEAD_DIM
KV_WIDTH = N_KV_HEADS * HEAD_DIM
WINDOW = 128
ROT_DIM = HEAD_DIM // 4
ROPE_THETA = 500000.0
NEG_BIG = -1e30
LRU_WIDTH = 1280
LRU_BLOCKS = 10
LRU_BLOCK_DIM = 128
LRU_C = 8.0
D_FF = 2816
EPS = 1e-6
IN_WIDTH = Q_WIDTH + 2 * KV_WIDTH + 2 * LRU_WIDTH + 2 * D_MODEL
O_Q = 0
O_K = O_Q + Q_WIDTH
O_V = O_K + KV_WIDTH
O_RX = O_V + KV_WIDTH
O_RG = O_RX + LRU_WIDTH
O_GA = O_RG + LRU_WIDTH
O_GR = O_GA + D_MODEL

LANES = 128
SUBLANES = 8
BATCH_GROUP = SUBLANES
VMEM_LIMIT = 56 * 1024 * 1024

BF16 = jnp.bfloat16
F32 = jnp.float32
LOG2E = 1.4426950408889634


def _const_spec(shape):
    nd = len(shape)
    return pl.BlockSpec(shape, lambda *_: (0,) * nd, pipeline_mode=pl.Buffered(1))


def _sigmoid(x):
    return 0.5 * jnp.tanh(0.5 * x) + 0.5


def _rms(x, w):
    ms = jnp.mean(x * x, axis=-1, keepdims=True)
    return x * lax.rsqrt(ms + EPS) * w


def _in_proj_kernel(x_ref, nw_ref, w_ref, rc_ref, rs1_ref, rs2_ref,
                    q_ref, k_ref, v_ref, xr_ref, rg_ref, ga_ref, gr_ref, *, tt):
    rows = BATCH_GROUP * tt
    x = x_ref[...].reshape(rows, D_MODEL)
    h = _rms(x, nw_ref[...]).astype(BF16)

    def proj(lo, width):
        return jnp.dot(h, w_ref[:, lo:lo + width], preferred_element_type=F32)

    rc = rc_ref[...][None]
    rs1 = rs1_ref[...][None]
    rs2 = rs2_ref[...][None]

    def rope(t):
        up = pltpu.roll(t, ROT_DIM // 2, axis=1).reshape(BATCH_GROUP, tt, HEAD_DIM)
        dn = pltpu.roll(t, HEAD_DIM - ROT_DIM // 2, axis=1).reshape(BATCH_GROUP, tt, HEAD_DIM)
        return t.reshape(BATCH_GROUP, tt, HEAD_DIM) * rc + up * rs1 + dn * rs2

    scale = LOG2E / math.sqrt(HEAD_DIM)
    zq = proj(O_Q, Q_WIDTH)
    for hh in range(N_Q_HEADS):
        sl = slice(hh * HEAD_DIM, (hh + 1) * HEAD_DIM)
        q_ref[:, :, sl] = (rope(zq[:, sl]) * scale).astype(BF16)
    zk = proj(O_K, KV_WIDTH)
    for hh in range(N_KV_HEADS):
        sl = slice(hh * HEAD_DIM, (hh + 1) * HEAD_DIM)
        k_ref[:, :, sl] = rope(zk[:, sl]).astype(BF16)
    v_ref[...] = proj(O_V, KV_WIDTH).astype(BF16).reshape(BATCH_GROUP, tt, KV_WIDTH)

    zx = proj(O_RX, LRU_WIDTH)
    for n in range(LRU_BLOCKS):
        for b in range(BATCH_GROUP):
            xr_ref[0, n, pl.ds(b, tt, stride=BATCH_GROUP), :] = (
                zx[b * tt:(b + 1) * tt, n * LANES:(n + 1) * LANES])

    rg_ref[...] = jax.nn.gelu(proj(O_RG, LRU_WIDTH)).astype(BF16).reshape(BATCH_GROUP, tt, LRU_WIDTH)
    ga_ref[...] = _sigmoid(proj(O_GA, D_MODEL)).astype(BF16).reshape(BATCH_GROUP, tt, D_MODEL)
    gr_ref[...] = _sigmoid(proj(O_GR, D_MODEL)).astype(BF16).reshape(BATCH_GROUP, tt, D_MODEL)


def _in_proj(x, norm_w, w_in, rope_c, rope_s1, rope_s2, *, tt):
    B, S, _ = x.shape
    nbg = B // BATCH_GROUP
    grid = (nbg, S // tt)

    def bm(width):
        return pl.BlockSpec((BATCH_GROUP, tt, width), lambda g, i: (g, i, 0))

    rope_spec = pl.BlockSpec((tt, HEAD_DIM), lambda g, i: (i, 0))
    out_shape = (
        jax.ShapeDtypeStruct((B, S, Q_WIDTH), BF16),
        jax.ShapeDtypeStruct((B, S, KV_WIDTH), BF16),
        jax.ShapeDtypeStruct((B, S, KV_WIDTH), BF16),
        jax.ShapeDtypeStruct((nbg, LRU_BLOCKS, S * BATCH_GROUP, LANES), F32),
        jax.ShapeDtypeStruct((B, S, LRU_WIDTH), BF16),
        jax.ShapeDtypeStruct((B, S, D_MODEL), BF16),
        jax.ShapeDtypeStruct((B, S, D_MODEL), BF16),
    )
    out_specs = (
        bm(Q_WIDTH), bm(KV_WIDTH), bm(KV_WIDTH),
        pl.BlockSpec((1, LRU_BLOCKS, tt * BATCH_GROUP, LANES), lambda g, i: (g, 0, i, 0)),
        bm(LRU_WIDTH), bm(D_MODEL), bm(D_MODEL),
    )
    return pl.pallas_call(
        functools.partial(_in_proj_kernel, tt=tt),
        grid=grid,
        in_specs=[bm(D_MODEL), _const_spec((1, D_MODEL)), _const_spec((D_MODEL, IN_WIDTH)),
                  rope_spec, rope_spec, rope_spec],
        out_specs=out_specs,
        out_shape=out_shape,
        compiler_params=pltpu.CompilerParams(
            dimension_semantics=("parallel", "parallel"), vmem_limit_bytes=VMEM_LIMIT),
        name="in_proj",
    )(x, norm_w, w_in, rope_c, rope_s1, rope_s2)


def _attn_kernel(sink_ref, q_ref, kp_ref, km_ref, kn_ref, vp_ref, vm_ref, vn_ref, o_ref, *, tq, nblk):
    i = pl.program_id(1)
    nsub = tq // WINDOW
    kcat = jnp.concatenate([kp_ref[0], km_ref[0], kn_ref[0]], axis=0)
    vcat = jnp.concatenate([vp_ref[0], vm_ref[0], vn_ref[0]], axis=0)
    rows = GROUP * WINDOW
    r = lax.broadcasted_iota(jnp.int32, (WINDOW, WINDOW), 0)
    c = lax.broadcasted_iota(jnp.int32, (WINDOW, WINDOW), 1)
    tri_prev = jnp.where(c >= r, 0.0, NEG_BIG).astype(F32)
    tri_next = jnp.where(c <= r, 0.0, NEG_BIG).astype(F32)
    head_of_row = lax.broadcasted_iota(jnp.int32, (rows, 1), 0) // WINDOW

    def banded(x, bias):
        return (x.reshape(GROUP, WINDOW, WINDOW) + bias[None]).reshape(rows, WINDOW)

    for j in range(nsub):
        blk = i * nsub + j
        bias_prev = jnp.where(blk == 0, NEG_BIG, tri_prev)
        bias_next = jnp.where(blk == nblk - 1, NEG_BIG, tri_next)
        for g in range(N_KV_HEADS):
            ks = kcat[j * WINDOW:(j + 3) * WINDOW, g * HEAD_DIM:(g + 1) * HEAD_DIM]
            vs = vcat[j * WINDOW:(j + 3) * WINDOW, g * HEAD_DIM:(g + 1) * HEAD_DIM]
            qs = jnp.concatenate(
                [q_ref[0, j * WINDOW:(j + 1) * WINDOW,
                       (g * GROUP + hh) * HEAD_DIM:(g * GROUP + hh + 1) * HEAD_DIM]
                 for hh in range(GROUP)], axis=0)
            s = lax.dot_general(qs, ks, (((1,), (1,)), ((), ())), preferred_element_type=F32)
            sp = banded(s[:, :WINDOW], bias_prev)
            sc = s[:, WINDOW:2 * WINDOW]
            sn = banded(s[:, 2 * WINDOW:], bias_next)
            sink = jnp.zeros((rows, 1), F32)
            for hh in range(GROUP):
                sink = jnp.where(head_of_row == hh, sink_ref[g * GROUP + hh] * LOG2E, sink)
            m = jnp.max(jnp.maximum(jnp.maximum(sp, sc), sn), axis=-1, keepdims=True)
            m = jnp.maximum(m, sink)
            pp, pc, pn = jnp.exp2(sp - m), jnp.exp2(sc - m), jnp.exp2(sn - m)
            denom = jnp.sum(pp + pc + pn, axis=-1, keepdims=True) + jnp.exp2(sink - m)
            p = jnp.concatenate([pp.astype(BF16), pc.astype(BF16), pn.astype(BF16)], axis=1)
            o = jnp.dot(p, vs, preferred_element_type=F32) / denom
            for hh in range(GROUP):
                col = (g * GROUP + hh) * HEAD_DIM
                o_ref[0, j * WINDOW:(j + 1) * WINDOW, col:col + HEAD_DIM] = (
                    o[hh * WINDOW:(hh + 1) * WINDOW].astype(BF16))


def _attention(q, k, v, sink, *, tq):
    B, S, _ = q.shape
    nblk = S // WINDOW
    nsub = tq // WINDOW
    grid = (B, S // tq)
    main = pl.BlockSpec((1, tq, KV_WIDTH), lambda b, i: (b, i, 0))
    prev = pl.BlockSpec((1, WINDOW, KV_WIDTH), lambda b, i: (b, jnp.maximum(i * nsub - 1, 0), 0))
    nxt = pl.BlockSpec((1, WINDOW, KV_WIDTH),
                       lambda b, i: (b, jnp.minimum((i + 1) * nsub, nblk - 1), 0))
    qspec = pl.BlockSpec((1, tq, Q_WIDTH), lambda b, i: (b, i, 0))
    return pl.pallas_call(
        functools.partial(_attn_kernel, tq=tq, nblk=nblk),
        grid=grid,
        in_specs=[pl.BlockSpec(memory_space=pltpu.SMEM), qspec, prev, main, nxt, prev, main, nxt],
        out_specs=qspec,
        out_shape=jax.ShapeDtypeStruct((B, S, Q_WIDTH), BF16),
        compiler_params=pltpu.CompilerParams(
            dimension_semantics=("parallel", "parallel"), vmem_limit_bytes=VMEM_LIMIT),
        name="attention",
    )(sink, q, k, k, k, v, v, v)


SLABS = 2


CONV_TAPS = 4
BIAS_PARTS = 2


def _rec_kernel(xm_ref, xp_ref, xn_ref, cw_ref, cb_ref, wg_ref, lam_ref, gate_ref,
                o_ref, hf_ref, rl_ref, carry_ref, *, tc, nchunk):
    j = pl.program_id(2)
    backward = j >= nchunk
    chunk = jnp.where(backward, 2 * nchunk - 1 - j, j)
    rows = tc * BATCH_GROUP

    @pl.when((j == 0) | (j == nchunk))
    def _():
        carry_ref[...] = jnp.zeros_like(carry_ref)

    has_prev = (chunk > 0).astype(F32)
    has_next = (chunk < nchunk - 1).astype(F32)
    base = chunk * rows
    lane = lax.broadcasted_iota(jnp.int32, (rows, LANES), 1)
    bias_cols = jnp.where(lane < BIAS_PARTS, 1.0, 0.0).astype(BF16)

    def coeffs(s):
        xe = jnp.concatenate(
            [xp_ref[0, s] * has_prev, xm_ref[0, s], xn_ref[0, s] * has_next], axis=0)
        cw = 0.5 * cw_ref[s]
        xh = 0.5 * cb_ref[s] + xe[0:rows] * cw[0:1]
        for k in range(1, CONV_TAPS):
            xh = xh + xe[k * BATCH_GROUP:k * BATCH_GROUP + rows] * cw[k:k + 1]
        lhs = jnp.concatenate([xh.astype(BF16), bias_cols], axis=1)
        g = jnp.dot(lhs, wg_ref[0, s], preferred_element_type=F32)
        tr = jnp.tanh(g[:, 0:LANES])
        ti = jnp.tanh(g[:, LANES:])
        c = (-0.5 * LRU_C * LOG2E) * jax.nn.softplus(-lam_ref[0, s])
        a = jnp.exp2(c * tr + c)
        y = 1.0 - a * a
        root = jnp.where(y > 0.0, y * lax.rsqrt(y), 0.0)
        u = root * ((ti + 1.0) * xh)
        return a, u

    def tile(v, t):
        return v[t * BATCH_GROUP:(t + 1) * BATCH_GROUP]

    def state_rows(t):
        return pl.ds(pl.multiple_of(base + t * BATCH_GROUP, BATCH_GROUP), BATCH_GROUP)

    @pl.when(jnp.logical_not(backward))
    def _():
        for s in range(SLABS):
            a, u = coeffs(s)
            h = carry_ref[s]
            for t in range(tc):
                h = tile(a, t) * h + tile(u, t)
                hf_ref[s, state_rows(t), :] = h
            carry_ref[s] = h

    @pl.when(backward)
    def _():
        for s in range(SLABS):
            a, u = coeffs(s)
            h = carry_ref[s]
            for t in reversed(range(tc)):
                h = tile(a, t) * h + tile(u, t)
                rl_ref[s, t * BATCH_GROUP:(t + 1) * BATCH_GROUP, :] = h + hf_ref[s, state_rows(t), :]
            carry_ref[s] = h
        for s in range(SLABS):
            for b in range(BATCH_GROUP):
                hb = rl_ref[s, pl.ds(b, tc, stride=BATCH_GROUP), :]
                sl = slice(s * LANES, (s + 1) * LANES)
                o_ref[b, :, sl] = (hb * gate_ref[b, :, sl].astype(F32)).astype(BF16)


def _recurrent(xr, gate, conv_w, conv_b, wg, lam, *, tc):
    nbg, _, rows_total, _ = xr.shape
    S = rows_total // BATCH_GROUP
    B = nbg * BATCH_GROUP
    nchunk = S // tc
    rows = tc * BATCH_GROUP
    grid = (nbg, LRU_BLOCKS // SLABS, 2 * nchunk)

    def chunk_of(j):
        return jnp.where(j >= nchunk, 2 * nchunk - 1 - j, j)

    def out_chunk(j):
        return jnp.where(j >= nchunk, 2 * nchunk - 1 - j, nchunk - 1)

    main = pl.BlockSpec((1, SLABS, rows, LANES), lambda g, c, j: (g, c, chunk_of(j), 0))
    prev = pl.BlockSpec(
        (1, SLABS, 2 * BATCH_GROUP, LANES),
        lambda g, c, j: (g, c, jnp.maximum(chunk_of(j) * (tc // 2) - 1, 0), 0))
    nxt = pl.BlockSpec(
        (1, SLABS, BATCH_GROUP, LANES),
        lambda g, c, j: (g, c, jnp.minimum((chunk_of(j) + 1) * tc, S - 1), 0))
    in_specs = [
        main, prev, nxt,
        pl.BlockSpec((SLABS, CONV_TAPS, LANES), lambda g, c, j: (c, 0, 0)),
        pl.BlockSpec((SLABS, 1, LANES), lambda g, c, j: (c, 0, 0)),
        pl.BlockSpec((1, SLABS, 2 * LANES, 2 * LANES), lambda g, c, j: (j // nchunk, c, 0, 0)),
        pl.BlockSpec((1, SLABS, 1, LANES), lambda g, c, j: (j // nchunk, c, 0, 0)),
        pl.BlockSpec((BATCH_GROUP, tc, SLABS * LANES), lambda g, c, j: (g, out_chunk(j), c)),
    ]
    return pl.pallas_call(
        functools.partial(_rec_kernel, tc=tc, nchunk=nchunk),
        grid=grid,
        in_specs=in_specs,
        out_specs=pl.BlockSpec((BATCH_GROUP, tc, SLABS * LANES),
                               lambda g, c, j: (g, out_chunk(j), c)),
        out_shape=jax.ShapeDtypeStruct((B, S, LRU_WIDTH), BF16),
        scratch_shapes=[
            pltpu.VMEM((SLABS, rows_total, LANES), F32),
            pltpu.VMEM((SLABS, rows, LANES), F32),
            pltpu.VMEM((SLABS, BATCH_GROUP, LANES), F32),
        ],
        compiler_params=pltpu.CompilerParams(
            dimension_semantics=("parallel", "parallel", "arbitrary"),
            vmem_limit_bytes=VMEM_LIMIT),
        name="recurrent",
    )(xr, xr, xr, conv_w, conv_b, wg, lam, gate)


def _merge_kernel(x_ref, at_ref, rc_ref, ga_ref, gr_ref, pa_ref, pr_ref, wo_ref, n1_ref, n2_ref,
                  x1_ref, hn_ref):
    half = x_ref.shape[0] // 2
    for part in range(2):
        rows = slice(part * half, (part + 1) * half)
        ya = jnp.dot(at_ref[rows, :], pa_ref[...], preferred_element_type=F32)
        yr = jnp.dot(rc_ref[rows, :], pr_ref[...], preferred_element_type=F32)
        merged = ga_ref[rows, :].astype(F32) * ya + gr_ref[rows, :].astype(F32) * yr
        mix = jnp.dot(merged.astype(BF16), wo_ref[...], preferred_element_type=F32)
        x1 = x_ref[rows, :] + _rms(mix, n1_ref[...])
        x1_ref[rows, :] = x1
        hn_ref[rows, :] = _rms(x1, n2_ref[...]).astype(BF16)


def _merge(x, attn, rec, ga, gr, pa, pr, wo, n1, n2, *, tm):
    T = x.shape[0]

    def tok(width):
        return pl.BlockSpec((tm, width), lambda i: (i, 0))

    return pl.pallas_call(
        _merge_kernel,
        grid=(T // tm,),
        in_specs=[tok(D_MODEL), tok(Q_WIDTH), tok(LRU_WIDTH), tok(D_MODEL), tok(D_MODEL),
                  _const_spec((Q_WIDTH, D_MODEL)), _const_spec((LRU_WIDTH, D_MODEL)),
                  _const_spec((D_MODEL, D_MODEL)), _const_spec((1, D_MODEL)),
                  _const_spec((1, D_MODEL))],
        out_specs=(tok(D_MODEL), tok(D_MODEL)),
        out_shape=(jax.ShapeDtypeStruct((T, D_MODEL), F32),
                   jax.ShapeDtypeStruct((T, D_MODEL), BF16)),
        compiler_params=pltpu.CompilerParams(
            dimension_semantics=("parallel",), vmem_limit_bytes=VMEM_LIMIT),
        name="merge",
    )(x, attn, rec, ga, gr, pa, pr, wo, n1, n2)


FF_CHUNK = 256


def _ffn_kernel(x1_ref, hn_ref, wi_ref, wo_ref, n_ref, o_ref, act_ref):
    h = hn_ref[...]
    for c in range(D_FF // FF_CHUNK):
        lo = c * FF_CHUNK
        g = jnp.dot(h, wi_ref[:, lo:lo + FF_CHUNK], preferred_element_type=F32)
        u = jnp.dot(h, wi_ref[:, D_FF + lo:D_FF + lo + FF_CHUNK], preferred_element_type=F32)
        act_ref[:, lo:lo + FF_CHUNK] = (g * _sigmoid(g) * u).astype(BF16)
    f = jnp.dot(act_ref[...], wo_ref[...], preferred_element_type=F32)
    o_ref[...] = x1_ref[...] + _rms(f, n_ref[...])


def _ffn(x1, hn, wi, wo, nw, *, tm):
    T = x1.shape[0]

    def tok(width):
        return pl.BlockSpec((tm, width), lambda i: (i, 0))

    return pl.pallas_call(
        _ffn_kernel,
        grid=(T // tm,),
        in_specs=[tok(D_MODEL), tok(D_MODEL), _const_spec((D_MODEL, 2 * D_FF)),
                  _const_spec((D_FF, D_MODEL)), _const_spec((1, D_MODEL))],
        out_specs=tok(D_MODEL),
        out_shape=jax.ShapeDtypeStruct((T, D_MODEL), F32),
        scratch_shapes=[pltpu.VMEM((tm, D_FF), BF16)],
        compiler_params=pltpu.CompilerParams(
            dimension_semantics=("parallel",), vmem_limit_bytes=VMEM_LIMIT),
        name="ffn",
    )(x1, hn, wi, wo, nw)


def _rope_tables(S):
    half = ROT_DIM // 2
    inv = ROPE_THETA ** (-jnp.arange(half, dtype=F32) / half)
    ang = jnp.arange(S).astype(F32)[:, None] * inv[None, :]
    cos, sin = jnp.cos(ang), jnp.sin(ang)
    zeros = jnp.zeros((S, HEAD_DIM - ROT_DIM), F32)
    zh = jnp.zeros((S, half), F32)
    rope_c = jnp.concatenate([cos, cos, jnp.ones_like(zeros)], axis=1)
    rope_s1 = jnp.concatenate([zh, sin, zeros], axis=1)
    rope_s2 = jnp.concatenate([-sin, zh, zeros], axis=1)
    return rope_c, rope_s1, rope_s2


def _pick(S, prefer):
    return prefer if S % prefer == 0 else S


def _layer(x, p):
    B, S, _ = x.shape
    tt = _pick(S, 64)
    q, k, v, xr, rg, ga, gr = _in_proj(x, p["norm_mix_pre"], p["w_in"], *_rope_tables(S), tt=tt)
    attn = _attention(q, k, v, p["attn_sink"], tq=_pick(S, 512))
    rec = _recurrent(xr, rg, p["conv_w"], p["conv_b"], p["wg"], p["lam"], tc=_pick(S, 256))
    T = B * S
    tm = _pick(T, 512)
    x1, hn = _merge(x.reshape(T, D_MODEL), attn.reshape(T, Q_WIDTH), rec.reshape(T, LRU_WIDTH),
                    ga.reshape(T, D_MODEL), gr.reshape(T, D_MODEL),
                    p["w_attn_proj"], p["w_rec_proj"], p["w_out"],
                    p["norm_mix_post"], p["norm_ffn_pre"], tm=tm)
    y = _ffn(x1, hn, p["w_ffn_in"], p["w_ffn_out"], p["norm_ffn_post"], tm=tm)
    return y.reshape(B, S, D_MODEL)


def _prep(norm_mix_pre, w_in, attn_sink, conv_w, conv_b, lru_w_a, lru_b_a, lru_w_i, lru_b_i,
          lru_lambda, w_attn_proj, w_rec_proj, w_out, norm_mix_post, norm_ffn_pre, w_ffn_in,
          w_ffn_out, norm_ffn_post):
    n_dirs = lru_w_a.shape[0]
    wg = jnp.concatenate([lru_w_a, lru_w_i], axis=-1).astype(BF16)
    bg = 0.5 * jnp.concatenate(
        [lru_b_a.reshape(n_dirs, LRU_BLOCKS, 1, LANES), lru_b_i.reshape(n_dirs, LRU_BLOCKS, 1, LANES)],
        axis=-1)
    bg_hi = bg.astype(BF16)
    bg_lo = (bg - bg_hi.astype(F32)).astype(BF16)
    pad = jnp.zeros((n_dirs, LRU_BLOCKS, LANES - BIAS_PARTS, 2 * LANES), BF16)
    wg = jnp.concatenate([wg, bg_hi, bg_lo, pad], axis=-2)
    return dict(
        norm_mix_pre=norm_mix_pre.reshape(1, D_MODEL),
        w_in=w_in.astype(BF16),
        attn_sink=attn_sink,
        conv_w=conv_w.reshape(CONV_TAPS, LRU_BLOCKS, LANES).transpose(1, 0, 2),
        conv_b=conv_b.reshape(LRU_BLOCKS, 1, LANES),
        wg=wg,
        lam=lru_lambda.reshape(n_dirs, LRU_BLOCKS, 1, LANES),
        w_attn_proj=w_attn_proj.astype(BF16),
        w_rec_proj=w_rec_proj.astype(BF16),
        w_out=w_out.astype(BF16),
        norm_mix_post=norm_mix_post.reshape(1, D_MODEL),
        norm_ffn_pre=norm_ffn_pre.reshape(1, D_MODEL),
        w_ffn_in=w_ffn_in.astype(BF16),
        w_ffn_out=w_ffn_out.astype(BF16),
        norm_ffn_post=norm_ffn_post.reshape(1, D_MODEL),
    )


def kernel(x_prompt, x_sample, norm_mix_pre, w_in, attn_sink, conv_w, conv_b, lru_w_a, lru_b_a,
           lru_w_i, lru_b_i, lru_lambda, w_attn_proj, w_rec_proj, w_out, norm_mix_post,
           norm_ffn_pre, w_ffn_in, w_ffn_out, norm_ffn_post):
    params = (norm_mix_pre, w_in, attn_sink, conv_w, conv_b, lru_w_a, lru_b_a, lru_w_i, lru_b_i,
              lru_lambda, w_attn_proj, w_rec_proj, w_out, norm_mix_post, norm_ffn_pre, w_ffn_in,
              w_ffn_out, norm_ffn_post)
    y_prompt, y_sample = x_prompt, x_sample
    for l in range(norm_mix_pre.shape[0]):
        p = _prep(*(a[l] for a in params))
        y_prompt = _layer(y_prompt, p)
        y_sample = _layer(y_sample, p)
    return (y_prompt, y_sample)
```

```python
import functools
import math

import jax
import jax.numpy as jnp
from jax import lax
from jax.experimental import pallas as pl
from jax.experimental.pallas import tpu as pltpu

D_MODEL = 1024
HEAD_DIM = 128
N_Q_HEADS = 8
N_KV_HEADS = 2
GROUP = N_Q_HEADS // N_KV_HEADS
Q_WIDTH = N_Q_HEADS * HEAD_DIM
KV_WIDTH = N_KV_HEADS * HEAD_DIM
WINDOW = 128
ROT_DIM = HEAD_DIM // 4
ROPE_THETA = 500000.0
NEG_BIG = -1e30
LRU_WIDTH = 1280
LRU_BLOCKS = 10
LRU_BLOCK_DIM = 128
LRU_C = 8.0
D_FF = 2816
EPS = 1e-6
IN_WIDTH = Q_WIDTH + 2 * KV_WIDTH + 2 * LRU_WIDTH + 2 * D_MODEL
O_Q = 0
O_K = O_Q + Q_WIDTH
O_V = O_K + KV_WIDTH
O_RX = O_V + KV_WIDTH
O_RG = O_RX + LRU_WIDTH
O_GA = O_RG + LRU_WIDTH
O_GR = O_GA + D_MODEL

LANES = 128
SUBLANES = 8
BATCH_GROUP = SUBLANES
VMEM_LIMIT = 56 * 1024 * 1024

BF16 = jnp.bfloat16
F32 = jnp.float32
LOG2E = 1.4426950408889634


def _const_spec(shape):
    nd = len(shape)
    return pl.BlockSpec(shape, lambda *_: (0,) * nd, pipeline_mode=pl.Buffered(1))


def _sigmoid(x):
    return 0.5 * jnp.tanh(0.5 * x) + 0.5


def _rms(x, w):
    ms = jnp.mean(x * x, axis=-1, keepdims=True)
    return x * lax.rsqrt(ms + EPS) * w


IN_PROJ_PARTS = 2


def _in_proj_kernel(x_ref, nw_ref, w_ref, rc_ref, rs1_ref, rs2_ref,
                    q_ref, k_ref, v_ref, xr_ref, rg_ref, ga_ref, gr_ref, *, tt):
    rc = rc_ref[...][None]
    rs1 = rs1_ref[...][None]
    rs2 = rs2_ref[...][None]
    scale = LOG2E / math.sqrt(HEAD_DIM)
    nb = BATCH_GROUP // IN_PROJ_PARTS
    rows = nb * tt

    def rope(t):
        up = pltpu.roll(t, ROT_DIM // 2, axis=1).reshape(nb, tt, HEAD_DIM)
        dn = pltpu.roll(t, HEAD_DIM - ROT_DIM // 2, axis=1).reshape(nb, tt, HEAD_DIM)
        return t.reshape(nb, tt, HEAD_DIM) * rc + up * rs1 + dn * rs2

    for part in range(IN_PROJ_PARTS):
        bs = slice(part * nb, (part + 1) * nb)
        x = x_ref[bs].reshape(rows, D_MODEL)
        h = _rms(x, nw_ref[...]).astype(BF16)

        def proj(lo, width):
            return jnp.dot(h, w_ref[:, lo:lo + width], preferred_element_type=F32)

        zq = proj(O_Q, Q_WIDTH)
        for hh in range(N_Q_HEADS):
            sl = slice(hh * HEAD_DIM, (hh + 1) * HEAD_DIM)
            q_ref[bs, :, sl] = (rope(zq[:, sl]) * scale).astype(BF16)
        zk = proj(O_K, KV_WIDTH)
        for hh in range(N_KV_HEADS):
            sl = slice(hh * HEAD_DIM, (hh + 1) * HEAD_DIM)
            k_ref[bs, :, sl] = rope(zk[:, sl]).astype(BF16)
        v_ref[bs] = proj(O_V, KV_WIDTH).astype(BF16).reshape(nb, tt, KV_WIDTH)

        zx = proj(O_RX, LRU_WIDTH)
        for n in range(LRU_BLOCKS):
            for b in range(nb):
                xr_ref[0, n, pl.ds(part * nb + b, tt, stride=BATCH_GROUP), :] = (
                    zx[b * tt:(b + 1) * tt, n * LANES:(n + 1) * LANES])

        rg_ref[bs] = jax.nn.gelu(proj(O_RG, LRU_WIDTH)).astype(BF16).reshape(nb, tt, LRU_WIDTH)
        ga_ref[bs] = _sigmoid(proj(O_GA, D_MODEL)).astype(BF16).reshape(nb, tt, D_MODEL)
        gr_ref[bs] = _sigmoid(proj(O_GR, D_MODEL)).astype(BF16).reshape(nb, tt, D_MODEL)


def _in_proj(x, norm_w, w_in, rope_c, rope_s1, rope_s2, *, tt):
    B, S, _ = x.shape
    nbg = B // BATCH_GROUP
    grid = (nbg, S // tt)

    def bm(width):
        return pl.BlockSpec((BATCH_GROUP, tt, width), lambda g, i: (g, i, 0))

    rope_spec = pl.BlockSpec((tt, HEAD_DIM), lambda g, i: (i, 0))
    out_shape = (
        jax.ShapeDtypeStruct((B, S, Q_WIDTH), BF16),
        jax.ShapeDtypeStruct((B, S, KV_WIDTH), BF16),
        jax.ShapeDtypeStruct((B, S, KV_WIDTH), BF16),
        jax.ShapeDtypeStruct((nbg, LRU_BLOCKS, S * BATCH_GROUP, LANES), F32),
        jax.ShapeDtypeStruct((B, S, LRU_WIDTH), BF16),
        jax.ShapeDtypeStruct((B, S, D_MODEL), BF16),
        jax.ShapeDtypeStruct((B, S, D_MODEL), BF16),
    )
    out_specs = (
        bm(Q_WIDTH), bm(KV_WIDTH), bm(KV_WIDTH),
        pl.BlockSpec((1, LRU_BLOCKS, tt * BATCH_GROUP, LANES), lambda g, i: (g, 0, i, 0)),
        bm(LRU_WIDTH), bm(D_MODEL), bm(D_MODEL),
    )
    return pl.pallas_call(
        functools.partial(_in_proj_kernel, tt=tt),
        grid=grid,
        in_specs=[bm(D_MODEL), _const_spec((1, D_MODEL)), _const_spec((D_MODEL, IN_WIDTH)),
                  rope_spec, rope_spec, rope_spec],
        out_specs=out_specs,
        out_shape=out_shape,
        compiler_params=pltpu.CompilerParams(
            dimension_semantics=("parallel", "parallel"), vmem_limit_bytes=VMEM_LIMIT),
        name="in_proj",
    )(x, norm_w, w_in, rope_c, rope_s1, rope_s2)


def _attn_kernel(sink_ref, q_ref, kp_ref, km_ref, kn_ref, vp_ref, vm_ref, vn_ref, o_ref, *, tq, nblk):
    i = pl.program_id(1)
    nsub = tq // WINDOW
    kcat = jnp.concatenate([kp_ref[0], km_ref[0], kn_ref[0]], axis=0)
    vcat = jnp.concatenate([vp_ref[0], vm_ref[0], vn_ref[0]], axis=0)
    rows = GROUP * WINDOW
    r = lax.broadcasted_iota(jnp.int32, (WINDOW, WINDOW), 0)
    c = lax.broadcasted_iota(jnp.int32, (WINDOW, WINDOW), 1)
    tri_prev = jnp.where(c >= r, 0.0, NEG_BIG).astype(F32)
    tri_next = jnp.where(c <= r, 0.0, NEG_BIG).astype(F32)
    head_of_row = lax.broadcasted_iota(jnp.int32, (rows, 1), 0) // WINDOW

    def banded(x, bias):
        return (x.reshape(GROUP, WINDOW, WINDOW) + bias[None]).reshape(rows, WINDOW)

    for j in range(nsub):
        blk = i * nsub + j
        bias_prev = jnp.where(blk == 0, NEG_BIG, tri_prev)
        bias_next = jnp.where(blk == nblk - 1, NEG_BIG, tri_next)
        for g in range(N_KV_HEADS):
            ks = kcat[j * WINDOW:(j + 3) * WINDOW, g * HEAD_DIM:(g + 1) * HEAD_DIM]
            vs = vcat[j * WINDOW:(j + 3) * WINDOW, g * HEAD_DIM:(g + 1) * HEAD_DIM]
            qs = jnp.concatenate(
                [q_ref[0, j * WINDOW:(j + 1) * WINDOW,
                       (g * GROUP + hh) * HEAD_DIM:(g * GROUP + hh + 1) * HEAD_DIM]
                 for hh in range(GROUP)], axis=0)
            s = lax.dot_general(qs, ks, (((1,), (1,)), ((), ())), preferred_element_type=F32)
            sp = banded(s[:, :WINDOW], bias_prev)
            sc = s[:, WINDOW:2 * WINDOW]
            sn = banded(s[:, 2 * WINDOW:], bias_next)
            sink = jnp.zeros((rows, 1), F32)
            for hh in range(GROUP):
                sink = jnp.where(head_of_row == hh, sink_ref[g * GROUP + hh] * LOG2E, sink)
            m = jnp.max(jnp.maximum(jnp.maximum(sp, sc), sn), axis=-1, keepdims=True)
            m = jnp.maximum(m, sink)
            pp, pc, pn = jnp.exp2(sp - m), jnp.exp2(sc - m), jnp.exp2(sn - m)
            denom = jnp.sum(pp + pc + pn, axis=-1, keepdims=True) + jnp.exp2(sink - m)
            p = jnp.concatenate([pp.astype(BF16), pc.astype(BF16), pn.astype(BF16)], axis=1)
            o = jnp.dot(p, vs, preferred_element_type=F32) / denom
            for hh in range(GROUP):
                col = (g * GROUP + hh) * HEAD_DIM
                o_ref[0, j * WINDOW:(j + 1) * WINDOW, col:col + HEAD_DIM] = (
                    o[hh * WINDOW:(hh + 1) * WINDOW].astype(BF16))


def _attention(q, k, v, sink, *, tq):
    B, S, _ = q.shape
    nblk = S // WINDOW
    nsub = tq // WINDOW
    grid = (B, S // tq)
    main = pl.BlockSpec((1, tq, KV_WIDTH), lambda b, i: (b, i, 0))
    prev = pl.BlockSpec((1, WINDOW, KV_WIDTH), lambda b, i: (b, jnp.maximum(i * nsub - 1, 0), 0))
    nxt = pl.BlockSpec((1, WINDOW, KV_WIDTH),
                       lambda b, i: (b, jnp.minimum((i + 1) * nsub, nblk - 1), 0))
    qspec = pl.BlockSpec((1, tq, Q_WIDTH), lambda b, i: (b, i, 0))
    return pl.pallas_call(
        functools.partial(_attn_kernel, tq=tq, nblk=nblk),
        grid=grid,
        in_specs=[pl.BlockSpec(memory_space=pltpu.SMEM), qspec, prev, main, nxt, prev, main, nxt],
        out_specs=qspec,
        out_shape=jax.ShapeDtypeStruct((B, S, Q_WIDTH), BF16),
        compiler_params=pltpu.CompilerParams(
            dimension_semantics=("parallel", "parallel"), vmem_limit_bytes=VMEM_LIMIT),
        name="attention",
    )(sink, q, k, k, k, v, v, v)


SLABS = 2


CONV_TAPS = 4
BIAS_PARTS = 2


def _rec_kernel(xm_ref, xp_ref, xn_ref, cw_ref, cb_ref, wg_ref, lam_ref, gate_ref,
                o_ref, hf_ref, rl_ref, carry_ref, *, tc, nchunk):
    j = pl.program_id(2)
    backward = j >= nchunk
    chunk = jnp.where(backward, 2 * nchunk - 1 - j, j)
    rows = tc * BATCH_GROUP

    @pl.when((j == 0) | (j == nchunk))
    def _():
        carry_ref[...] = jnp.zeros_like(carry_ref)

    has_prev = (chunk > 0).astype(F32)
    has_next = (chunk < nchunk - 1).astype(F32)
    base = chunk * rows
    lane = lax.broadcasted_iota(jnp.int32, (rows, LANES), 1)
    bias_cols = jnp.where(lane < BIAS_PARTS, 1.0, 0.0).astype(BF16)

    def coeffs(s):
        xe = jnp.concatenate(
            [xp_ref[0, s] * has_prev, xm_ref[0, s], xn_ref[0, s] * has_next], axis=0)
        cw = 0.5 * cw_ref[s]
        xh = 0.5 * cb_ref[s] + xe[0:rows] * cw[0:1]
        for k in range(1, CONV_TAPS):
            xh = xh + xe[k * BATCH_GROUP:k * BATCH_GROUP + rows] * cw[k:k + 1]
        lhs = jnp.concatenate([xh.astype(BF16), bias_cols], axis=1)
        g = jnp.dot(lhs, wg_ref[0, s], preferred_element_type=F32)
        tr = jnp.tanh(g[:, 0:LANES])
        ti = jnp.tanh(g[:, LANES:])
        c = (-0.5 * LRU_C * LOG2E) * jax.nn.softplus(-lam_ref[0, s])
        a = jnp.exp2(c * tr + c)
        y = 1.0 - a * a
        root = jnp.where(y > 0.0, y * lax.rsqrt(y), 0.0)
        u = root * ((ti + 1.0) * xh)
        return a, u

    def tile(v, t):
        return v[t * BATCH_GROUP:(t + 1) * BATCH_GROUP]

    def state_rows(t):
        return pl.ds(pl.multiple_of(base + t * BATCH_GROUP, BATCH_GROUP), BATCH_GROUP)

    @pl.when(jnp.logical_not(backward))
    def _():
        for s in range(SLABS):
            a, u = coeffs(s)
            h = carry_ref[s]
            for t in range(tc):
                h = tile(a, t) * h + tile(u, t)
                hf_ref[s, state_rows(t), :] = h
            carry_ref[s] = h

    @pl.when(backward)
    def _():
        for s in range(SLABS):
            a, u = coeffs(s)
            h = carry_ref[s]
            for t in reversed(range(tc)):
                h = tile(a, t) * h + tile(u, t)
                rl_ref[s, t * BATCH_GROUP:(t + 1) * BATCH_GROUP, :] = h + hf_ref[s, state_rows(t), :]
            carry_ref[s] = h
        for s in range(SLABS):
            for b in range(BATCH_GROUP):
                hb = rl_ref[s, pl.ds(b, tc, stride=BATCH_GROUP), :]
                sl = slice(s * LANES, (s + 1) * LANES)
                o_ref[b, :, sl] = (hb * gate_ref[b, :, sl].astype(F32)).astype(BF16)


def _recurrent(xr, gate, conv_w, conv_b, wg, lam, *, tc):
    nbg, _, rows_total, _ = xr.shape
    S = rows_total // BATCH_GROUP
    B = nbg * BATCH_GROUP
    nchunk = S // tc
    rows = tc * BATCH_GROUP
    grid = (nbg, LRU_BLOCKS // SLABS, 2 * nchunk)

    def chunk_of(j):
        return jnp.where(j >= nchunk, 2 * nchunk - 1 - j, j)

    def out_chunk(j):
        return jnp.where(j >= nchunk, 2 * nchunk - 1 - j, nchunk - 1)

    main = pl.BlockSpec((1, SLABS, rows, LANES), lambda g, c, j: (g, c, chunk_of(j), 0))
    prev = pl.BlockSpec(
        (1, SLABS, 2 * BATCH_GROUP, LANES),
        lambda g, c, j: (g, c, jnp.maximum(chunk_of(j) * (tc // 2) - 1, 0), 0))
    nxt = pl.BlockSpec(
        (1, SLABS, BATCH_GROUP, LANES),
        lambda g, c, j: (g, c, jnp.minimum((chunk_of(j) + 1) * tc, S - 1), 0))
    in_specs = [
        main, prev, nxt,
        pl.BlockSpec((SLABS, CONV_TAPS, LANES), lambda g, c, j: (c, 0, 0)),
        pl.BlockSpec((SLABS, 1, LANES), lambda g, c, j: (c, 0, 0)),
        pl.BlockSpec((1, SLABS, 2 * LANES, 2 * LANES), lambda g, c, j: (j // nchunk, c, 0, 0)),
        pl.BlockSpec((1, SLABS, 1, LANES), lambda g, c, j: (j // nchunk, c, 0, 0)),
        pl.BlockSpec((BATCH_GROUP, tc, SLABS * LANES), lambda g, c, j: (g, out_chunk(j), c)),
    ]
    return pl.pallas_call(
        functools.partial(_rec_kernel, tc=tc, nchunk=nchunk),
        grid=grid,
        in_specs=in_specs,
        out_specs=pl.BlockSpec((BATCH_GROUP, tc, SLABS * LANES),
                               lambda g, c, j: (g, out_chunk(j), c)),
        out_shape=jax.ShapeDtypeStruct((B, S, LRU_WIDTH), BF16),
        scratch_shapes=[
            pltpu.VMEM((SLABS, rows_total, LANES), F32),
            pltpu.VMEM((SLABS, rows, LANES), F32),
            pltpu.VMEM((SLABS, BATCH_GROUP, LANES), F32),
        ],
        compiler_params=pltpu.CompilerParams(
            dimension_semantics=("parallel", "parallel", "arbitrary"),
            vmem_limit_bytes=VMEM_LIMIT),
        name="recurrent",
    )(xr, xr, xr, conv_w, conv_b, wg, lam, gate)


def _merge_kernel(x_ref, at_ref, rc_ref, ga_ref, gr_ref, pa_ref, pr_ref, wo_ref, n1_ref, n2_ref,
                  x1_ref, hn_ref):
    half = x_ref.shape[0] // 2
    for part in range(2):
        rows = slice(part * half, (part + 1) * half)
        ya = jnp.dot(at_ref[rows, :], pa_ref[...], preferred_element_type=F32)
        yr = jnp.dot(rc_ref[rows, :], pr_ref[...], preferred_element_type=F32)
        merged = ga_ref[rows, :].astype(F32) * ya + gr_ref[rows, :].astype(F32) * yr
        mix = jnp.dot(merged.astype(BF16), wo_ref[...], preferred_element_type=F32)
        x1 = x_ref[rows, :] + _rms(mix, n1_ref[...])
        x1_ref[rows, :] = x1
        hn_ref[rows, :] = _rms(x1, n2_ref[...]).astype(BF16)


def _merge(x, attn, rec, ga, gr, pa, pr, wo, n1, n2, *, tm):
    T = x.shape[0]

    def tok(width):
        return pl.BlockSpec((tm, width), lambda i: (i, 0))

    return pl.pallas_call(
        _merge_kernel,
        grid=(T // tm,),
        in_specs=[tok(D_MODEL), tok(Q_WIDTH), tok(LRU_WIDTH), tok(D_MODEL), tok(D_MODEL),
                  _const_spec((Q_WIDTH, D_MODEL)), _const_spec((LRU_WIDTH, D_MODEL)),
                  _const_spec((D_MODEL, D_MODEL)), _const_spec((1, D_MODEL)),
                  _const_spec((1, D_MODEL))],
        out_specs=(tok(D_MODEL), tok(D_MODEL)),
        out_shape=(jax.ShapeDtypeStruct((T, D_MODEL), F32),
                   jax.ShapeDtypeStruct((T, D_MODEL), BF16)),
        compiler_params=pltpu.CompilerParams(
            dimension_semantics=("parallel",), vmem_limit_bytes=VMEM_LIMIT),
        name="merge",
    )(x, attn, rec, ga, gr, pa, pr, wo, n1, n2)


FF_CHUNK = 256


def _ffn_kernel(x1_ref, hn_ref, wi_ref, wo_ref, n_ref, o_ref, act_ref):
    half = hn_ref.shape[0] // 2
    for part in range(2):
        rows = slice(part * half, (part + 1) * half)
        h = hn_ref[rows, :]
        for c in range(D_FF // FF_CHUNK):
            lo = c * FF_CHUNK
            g = jnp.dot(h, wi_ref[:, lo:lo + FF_CHUNK], preferred_element_type=F32)
            u = jnp.dot(h, wi_ref[:, D_FF + lo:D_FF + lo + FF_CHUNK], preferred_element_type=F32)
            act_ref[rows, lo:lo + FF_CHUNK] = (g * _sigmoid(g) * u).astype(BF16)
        f = jnp.dot(act_ref[rows, :], wo_ref[...], preferred_element_type=F32)
        o_ref[rows, :] = x1_ref[rows, :] + _rms(f, n_ref[...])


def _ffn(x1, hn, wi, wo, nw, *, tm):
    T = x1.shape[0]

    def tok(width):
        return pl.BlockSpec((tm, width), lambda i: (i, 0))

    return pl.pallas_call(
        _ffn_kernel,
        grid=(T // tm,),
        in_specs=[tok(D_MODEL), tok(D_MODEL), _const_spec((D_MODEL, 2 * D_FF)),
                  _const_spec((D_FF, D_MODEL)), _const_spec((1, D_MODEL))],
        out_specs=tok(D_MODEL),
        out_shape=jax.ShapeDtypeStruct((T, D_MODEL), F32),
        scratch_shapes=[pltpu.VMEM((tm, D_FF), BF16)],
        compiler_params=pltpu.CompilerParams(
            dimension_semantics=("parallel",), vmem_limit_bytes=VMEM_LIMIT),
        name="ffn",
    )(x1, hn, wi, wo, nw)


def _rope_tables(S):
    half = ROT_DIM // 2
    inv = ROPE_THETA ** (-jnp.arange(half, dtype=F32) / half)
    ang = jnp.arange(S).astype(F32)[:, None] * inv[None, :]
    cos, sin = jnp.cos(ang), jnp.sin(ang)
    zeros = jnp.zeros((S, HEAD_DIM - ROT_DIM), F32)
    zh = jnp.zeros((S, half), F32)
    rope_c = jnp.concatenate([cos, cos, jnp.ones_like(zeros)], axis=1)
    rope_s1 = jnp.concatenate([zh, sin, zeros], axis=1)
    rope_s2 = jnp.concatenate([-sin, zh, zeros], axis=1)
    return rope_c, rope_s1, rope_s2


def _pick(S, prefer):
    return prefer if S % prefer == 0 else S


def _layer(x, p):
    B, S, _ = x.shape
    tt = _pick(S, 64)
    q, k, v, xr, rg, ga, gr = _in_proj(x, p["norm_mix_pre"], p["w_in"], *_rope_tables(S), tt=tt)
    attn = _attention(q, k, v, p["attn_sink"], tq=_pick(S, 512))
    rec = _recurrent(xr, rg, p["conv_w"], p["conv_b"], p["wg"], p["lam"], tc=_pick(S, 256))
    T = B * S
    tm = _pick(T, 512)
    x1, hn = _merge(x.reshape(T, D_MODEL), attn.reshape(T, Q_WIDTH), rec.reshape(T, LRU_WIDTH),
                    ga.reshape(T, D_MODEL), gr.reshape(T, D_MODEL),
                    p["w_attn_proj"], p["w_rec_proj"], p["w_out"],
                    p["norm_mix_post"], p["norm_ffn_pre"], tm=tm)
    y = _ffn(x1, hn, p["w_ffn_in"], p["w_ffn_out"], p["norm_ffn_post"], tm=tm)
    return y.reshape(B, S, D_MODEL)


def _prep(norm_mix_pre, w_in, attn_sink, conv_w, conv_b, lru_w_a, lru_b_a, lru_w_i, lru_b_i,
          lru_lambda, w_attn_proj, w_rec_proj, w_out, norm_mix_post, norm_ffn_pre, w_ffn_in,
          w_ffn_out, norm_ffn_post):
    n_dirs = lru_w_a.shape[0]
    wg = jnp.concatenate([lru_w_a, lru_w_i], axis=-1).astype(BF16)
    bg = 0.5 * jnp.concatenate(
        [lru_b_a.reshape(n_dirs, LRU_BLOCKS, 1, LANES), lru_b_i.reshape(n_dirs, LRU_BLOCKS, 1, LANES)],
        axis=-1)
    bg_hi = bg.astype(BF16)
    bg_lo = (bg - bg_hi.astype(F32)).astype(BF16)
    pad = jnp.zeros((n_dirs, LRU_BLOCKS, LANES - BIAS_PARTS, 2 * LANES), BF16)
    wg = jnp.concatenate([wg, bg_hi, bg_lo, pad], axis=-2)
    return dict(
        norm_mix_pre=norm_mix_pre.reshape(1, D_MODEL),
        w_in=w_in.astype(BF16),
        attn_sink=attn_sink,
        conv_w=conv_w.reshape(CONV_TAPS, LRU_BLOCKS, LANES).transpose(1, 0, 2),
        conv_b=conv_b.reshape(LRU_BLOCKS, 1, LANES),
        wg=wg,
        lam=lru_lambda.reshape(n_dirs, LRU_BLOCKS, 1, LANES),
        w_attn_proj=w_attn_proj.astype(BF16),
        w_rec_proj=w_rec_proj.astype(BF16),
        w_out=w_out.astype(BF16),
        norm_mix_post=norm_mix_post.reshape(1, D_MODEL),
        norm_ffn_pre=norm_ffn_pre.reshape(1, D_MODEL),
        w_ffn_in=w_ffn_in.astype(BF16),
        w_ffn_out=w_ffn_out.astype(BF16),
        norm_ffn_post=norm_ffn_post.reshape(1, D_MODEL),
    )


def kernel(x_prompt, x_sample, norm_mix_pre, w_in, attn_sink, conv_w, conv_b, lru_w_a, lru_b_a,
           lru_w_i, lru_b_i, lru_lambda, w_attn_proj, w_rec_proj, w_out, norm_mix_post,
           norm_ffn_pre, w_ffn_in, w_ffn_out, norm_ffn_post):
    params = (norm_mix_pre, w_in, attn_sink, conv_w, conv_b, lru_w_a, lru_b_a, lru_w_i, lru_b_i,
              lru_lambda, w_attn_proj, w_rec_proj, w_out, norm_mix_post, norm_ffn_pre, w_ffn_in,
              w_ffn_out, norm_ffn_post)
    y_prompt, y_sample = x_prompt, x_sample
    for l in range(norm_mix_pre.shape[0]):
        p = _prep(*(a[l] for a in params))
        y_prompt = _layer(y_prompt, p)
        y_sample = _layer(y_sample, p)
    return (y_prompt, y_sample)
```

```python
import functools
import math

import jax
import jax.numpy as jnp
from jax import lax
from jax.experimental import pallas as pl
from jax.experimental.pallas import tpu as pltpu

D_MODEL = 1024
HEAD_DIM = 128
N_Q_HEADS = 8
N_KV_HEADS = 2
GROUP = N_Q_HEADS // N_KV_HEADS
Q_WIDTH = N_Q_HEADS * HEAD_DIM
KV_WIDTH = N_KV_HEADS * HEAD_DIM
WINDOW = 128
ROT_DIM = HEAD_DIM // 4
ROPE_THETA = 500000.0
NEG_BIG = -1e30
LRU_WIDTH = 1280
LRU_BLOCKS = 10
LRU_BLOCK_DIM = 128
LRU_C = 8.0
D_FF = 2816
EPS = 1e-6
IN_WIDTH = Q_WIDTH + 2 * KV_WIDTH + 2 * LRU_WIDTH + 2 * D_MODEL
O_Q = 0
O_K = O_Q + Q_WIDTH
O_V = O_K + KV_WIDTH
O_RX = O_V + KV_WIDTH
O_RG = O_RX + LRU_WIDTH
O_GA = O_RG + LRU_WIDTH
O_GR = O_GA + D_MODEL

LANES = 128
SUBLANES = 8
BATCH_GROUP = SUBLANES
VMEM_LIMIT = 56 * 1024 * 1024

BF16 = jnp.bfloat16
F32 = jnp.float32
LOG2E = 1.4426950408889634


def _const_spec(shape):
    nd = len(shape)
    return pl.BlockSpec(shape, lambda *_: (0,) * nd, pipeline_mode=pl.Buffered(1))


def _sigmoid(x):
    return 0.5 * jnp.tanh(0.5 * x) + 0.5


def _rms(x, w):
    ms = jnp.mean(x * x, axis=-1, keepdims=True)
    return x * lax.rsqrt(ms + EPS) * w


IN_PROJ_PARTS = 1


def _in_proj_kernel(x_ref, nw_ref, w_ref, rc_ref, rs1_ref, rs2_ref,
                    q_ref, k_ref, v_ref, xr_ref, rg_ref, ga_ref, gr_ref, *, tt):
    rc = rc_ref[...][None]
    rs1 = rs1_ref[...][None]
    rs2 = rs2_ref[...][None]
    scale = LOG2E / math.sqrt(HEAD_DIM)
    nb = BATCH_GROUP // IN_PROJ_PARTS
    rows = nb * tt

    def rope(t):
        up = pltpu.roll(t, ROT_DIM // 2, axis=1).reshape(nb, tt, HEAD_DIM)
        dn = pltpu.roll(t, HEAD_DIM - ROT_DIM // 2, axis=1).reshape(nb, tt, HEAD_DIM)
        return t.reshape(nb, tt, HEAD_DIM) * rc + up * rs1 + dn * rs2

    for part in range(IN_PROJ_PARTS):
        bs = slice(part * nb, (part + 1) * nb)
        x = x_ref[bs].reshape(rows, D_MODEL)
        h = _rms(x, nw_ref[...]).astype(BF16)

        def proj(lo, width):
            return jnp.dot(h, w_ref[:, lo:lo + width], preferred_element_type=F32)

        zq = proj(O_Q, Q_WIDTH)
        for hh in range(N_Q_HEADS):
            sl = slice(hh * HEAD_DIM, (hh + 1) * HEAD_DIM)
            q_ref[bs, :, sl] = (rope(zq[:, sl]) * scale).astype(BF16)
        zk = proj(O_K, KV_WIDTH)
        for hh in range(N_KV_HEADS):
            sl = slice(hh * HEAD_DIM, (hh + 1) * HEAD_DIM)
            k_ref[bs, :, sl] = rope(zk[:, sl]).astype(BF16)
        v_ref[bs] = proj(O_V, KV_WIDTH).astype(BF16).reshape(nb, tt, KV_WIDTH)

        zx = proj(O_RX, LRU_WIDTH)
        for n in range(LRU_BLOCKS):
            for b in range(nb):
                xr_ref[0, n, pl.ds(part * nb + b, tt, stride=BATCH_GROUP), :] = (
                    zx[b * tt:(b + 1) * tt, n * LANES:(n + 1) * LANES])

        rg_ref[bs] = jax.nn.gelu(proj(O_RG, LRU_WIDTH)).astype(BF16).reshape(nb, tt, LRU_WIDTH)
        ga_ref[bs] = _sigmoid(proj(O_GA, D_MODEL)).astype(BF16).reshape(nb, tt, D_MODEL)
        gr_ref[bs] = _sigmoid(proj(O_GR, D_MODEL)).astype(BF16).reshape(nb, tt, D_MODEL)


def _in_proj(x, norm_w, w_in, rope_c, rope_s1, rope_s2, *, tt):
    B, S, _ = x.shape
    nbg = B // BATCH_GROUP
    grid = (nbg, S // tt)

    def bm(width):
        return pl.BlockSpec((BATCH_GROUP, tt, width), lambda g, i: (g, i, 0))

    rope_spec = pl.BlockSpec((tt, HEAD_DIM), lambda g, i: (i, 0))
    out_shape = (
        jax.ShapeDtypeStruct((B, S, Q_WIDTH), BF16),
        jax.ShapeDtypeStruct((B, S, KV_WIDTH), BF16),
        jax.ShapeDtypeStruct((B, S, KV_WIDTH), BF16),
        jax.ShapeDtypeStruct((nbg, LRU_BLOCKS, S * BATCH_GROUP, LANES), F32),
        jax.ShapeDtypeStruct((B, S, LRU_WIDTH), BF16),
        jax.ShapeDtypeStruct((B, S, D_MODEL), BF16),
        jax.ShapeDtypeStruct((B, S, D_MODEL), BF16),
    )
    out_specs = (
        bm(Q_WIDTH), bm(KV_WIDTH), bm(KV_WIDTH),
        pl.BlockSpec((1, LRU_BLOCKS, tt * BATCH_GROUP, LANES), lambda g, i: (g, 0, i, 0)),
        bm(LRU_WIDTH), bm(D_MODEL), bm(D_MODEL),
    )
    return pl.pallas_call(
        functools.partial(_in_proj_kernel, tt=tt),
        grid=grid,
        in_specs=[bm(D_MODEL), _const_spec((1, D_MODEL)), _const_spec((D_MODEL, IN_WIDTH)),
                  rope_spec, rope_spec, rope_spec],
        out_specs=out_specs,
        out_shape=out_shape,
        compiler_params=pltpu.CompilerParams(
            dimension_semantics=("parallel", "parallel"), vmem_limit_bytes=VMEM_LIMIT),
        name="in_proj",
    )(x, norm_w, w_in, rope_c, rope_s1, rope_s2)


def _attn_kernel(sink_ref, q_ref, kp_ref, km_ref, kn_ref, vp_ref, vm_ref, vn_ref, o_ref, *, tq, nblk):
    i = pl.program_id(1)
    nsub = tq // WINDOW
    kcat = jnp.concatenate([kp_ref[0], km_ref[0], kn_ref[0]], axis=0)
    vcat = jnp.concatenate([vp_ref[0], vm_ref[0], vn_ref[0]], axis=0)
    rows = GROUP * WINDOW
    r = lax.broadcasted_iota(jnp.int32, (WINDOW, WINDOW), 0)
    c = lax.broadcasted_iota(jnp.int32, (WINDOW, WINDOW), 1)
    tri_prev = jnp.where(c >= r, 0.0, NEG_BIG).astype(F32)
    tri_next = jnp.where(c <= r, 0.0, NEG_BIG).astype(F32)
    head_of_row = lax.broadcasted_iota(jnp.int32, (rows, 1), 0) // WINDOW

    def banded(x, bias):
        return (x.reshape(GROUP, WINDOW, WINDOW) + bias[None]).reshape(rows, WINDOW)

    for j in range(nsub):
        blk = i * nsub + j
        bias_prev = jnp.where(blk == 0, NEG_BIG, tri_prev)
        bias_next = jnp.where(blk == nblk - 1, NEG_BIG, tri_next)
        for g in range(N_KV_HEADS):
            ks = kcat[j * WINDOW:(j + 3) * WINDOW, g * HEAD_DIM:(g + 1) * HEAD_DIM]
            vs = vcat[j * WINDOW:(j + 3) * WINDOW, g * HEAD_DIM:(g + 1) * HEAD_DIM]
            qs = jnp.concatenate(
                [q_ref[0, j * WINDOW:(j + 1) * WINDOW,
                       (g * GROUP + hh) * HEAD_DIM:(g * GROUP + hh + 1) * HEAD_DIM]
                 for hh in range(GROUP)], axis=0)
            s = lax.dot_general(qs, ks, (((1,), (1,)), ((), ())), preferred_element_type=F32)
            sp = banded(s[:, :WINDOW], bias_prev)
            sc = s[:, WINDOW:2 * WINDOW]
            sn = banded(s[:, 2 * WINDOW:], bias_next)
            sink = jnp.zeros((rows, 1), F32)
            for hh in range(GROUP):
                sink = jnp.where(head_of_row == hh, sink_ref[g * GROUP + hh] * LOG2E, sink)
            m = jnp.max(jnp.maximum(jnp.maximum(sp, sc), sn), axis=-1, keepdims=True)
            m = jnp.maximum(m, sink)
            pp, pc, pn = jnp.exp2(sp - m), jnp.exp2(sc - m), jnp.exp2(sn - m)
            denom = jnp.sum(pp + pc + pn, axis=-1, keepdims=True) + jnp.exp2(sink - m)
            p = jnp.concatenate([pp.astype(BF16), pc.astype(BF16), pn.astype(BF16)], axis=1)
            o = jnp.dot(p, vs, preferred_element_type=F32) / denom
            for hh in range(GROUP):
                col = (g * GROUP + hh) * HEAD_DIM
                o_ref[0, j * WINDOW:(j + 1) * WINDOW, col:col + HEAD_DIM] = (
                    o[hh * WINDOW:(hh + 1) * WINDOW].astype(BF16))


def _attention(q, k, v, sink, *, tq):
    B, S, _ = q.shape
    nblk = S // WINDOW
    nsub = tq // WINDOW
    grid = (B, S // tq)
    main = pl.BlockSpec((1, tq, KV_WIDTH), lambda b, i: (b, i, 0))
    prev = pl.BlockSpec((1, WINDOW, KV_WIDTH), lambda b, i: (b, jnp.maximum(i * nsub - 1, 0), 0))
    nxt = pl.BlockSpec((1, WINDOW, KV_WIDTH),
                       lambda b, i: (b, jnp.minimum((i + 1) * nsub, nblk - 1), 0))
    qspec = pl.BlockSpec((1, tq, Q_WIDTH), lambda b, i: (b, i, 0))
    return pl.pallas_call(
        functools.partial(_attn_kernel, tq=tq, nblk=nblk),
        grid=grid,
        in_specs=[pl.BlockSpec(memory_space=pltpu.SMEM), qspec, prev, main, nxt, prev, main, nxt],
        out_specs=qspec,
        out_shape=jax.ShapeDtypeStruct((B, S, Q_WIDTH), BF16),
        compiler_params=pltpu.CompilerParams(
            dimension_semantics=("parallel", "parallel"), vmem_limit_bytes=VMEM_LIMIT),
        name="attention",
    )(sink, q, k, k, k, v, v, v)


SLABS = 2


CONV_TAPS = 4
BIAS_PARTS = 2


def _rec_kernel(xm_ref, xp_ref, xn_ref, cw_ref, cb_ref, wg_ref, lam_ref, gate_ref,
                o_ref, hf_ref, rl_ref, carry_ref, *, tc, nchunk):
    j = pl.program_id(2)
    backward = j >= nchunk
    chunk = jnp.where(backward, 2 * nchunk - 1 - j, j)
    rows = tc * BATCH_GROUP

    @pl.when((j == 0) | (j == nchunk))
    def _():
        carry_ref[...] = jnp.zeros_like(carry_ref)

    has_prev = (chunk > 0).astype(F32)
    has_next = (chunk < nchunk - 1).astype(F32)
    base = chunk * rows
    lane = lax.broadcasted_iota(jnp.int32, (rows, LANES), 1)
    bias_cols = jnp.where(lane < BIAS_PARTS, 1.0, 0.0).astype(BF16)

    def coeffs(s):
        xe = jnp.concatenate(
            [xp_ref[0, s] * has_prev, xm_ref[0, s], xn_ref[0, s] * has_next], axis=0)
        cw = 0.5 * cw_ref[s]
        xh = 0.5 * cb_ref[s] + xe[0:rows] * cw[0:1]
        for k in range(1, CONV_TAPS):
            xh = xh + xe[k * BATCH_GROUP:k * BATCH_GROUP + rows] * cw[k:k + 1]
        lhs = jnp.concatenate([xh.astype(BF16), bias_cols], axis=1)
        g = jnp.dot(lhs, wg_ref[0, s], preferred_element_type=F32)
        tr = jnp.tanh(g[:, 0:LANES])
        ti = jnp.tanh(g[:, LANES:])
        c = (-0.5 * LRU_C * LOG2E) * jax.nn.softplus(-lam_ref[0, s])
        a = jnp.exp2(c * tr + c)
        y = 1.0 - a * a
        root = jnp.where(y > 0.0, y * lax.rsqrt(y), 0.0)
        u = root * ((ti + 1.0) * xh)
        return a, u

    def tile(v, t):
        return v[t * BATCH_GROUP:(t + 1) * BATCH_GROUP]

    def state_rows(t):
        return pl.ds(pl.multiple_of(base + t * BATCH_GROUP, BATCH_GROUP), BATCH_GROUP)

    @pl.when(jnp.logical_not(backward))
    def _():
        for s in range(SLABS):
            a, u = coeffs(s)
            h = carry_ref[s]
            for t in range(tc):
                h = tile(a, t) * h + tile(u, t)
                hf_ref[s, state_rows(t), :] = h
            carry_ref[s] = h

    @pl.when(backward)
    def _():
        for s in range(SLABS):
            a, u = coeffs(s)
            h = carry_ref[s]
            for t in reversed(range(tc)):
                h = tile(a, t) * h + tile(u, t)
                rl_ref[s, t * BATCH_GROUP:(t + 1) * BATCH_GROUP, :] = h + hf_ref[s, state_rows(t), :]
            carry_ref[s] = h
        for s in range(SLABS):
            for b in range(BATCH_GROUP):
                hb = rl_ref[s, pl.ds(b, tc, stride=BATCH_GROUP), :]
                sl = slice(s * LANES, (s + 1) * LANES)
                o_ref[b, :, sl] = (hb * gate_ref[b, :, sl].astype(F32)).astype(BF16)


def _recurrent(xr, gate, conv_w, conv_b, wg, lam, *, tc):
    nbg, _, rows_total, _ = xr.shape
    S = rows_total // BATCH_GROUP
    B = nbg * BATCH_GROUP
    nchunk = S // tc
    rows = tc * BATCH_GROUP
    grid = (nbg, LRU_BLOCKS // SLABS, 2 * nchunk)

    def chunk_of(j):
        return jnp.where(j >= nchunk, 2 * nchunk - 1 - j, j)

    def out_chunk(j):
        return jnp.where(j >= nchunk, 2 * nchunk - 1 - j, nchunk - 1)

    main = pl.BlockSpec((1, SLABS, rows, LANES), lambda g, c, j: (g, c, chunk_of(j), 0))
    prev = pl.BlockSpec(
        (1, SLABS, 2 * BATCH_GROUP, LANES),
        lambda g, c, j: (g, c, jnp.maximum(chunk_of(j) * (tc // 2) - 1, 0), 0))
    nxt = pl.BlockSpec(
        (1, SLABS, BATCH_GROUP, LANES),
        lambda g, c, j: (g, c, jnp.minimum((chunk_of(j) + 1) * tc, S - 1), 0))
    in_specs = [
        main, prev, nxt,
        pl.BlockSpec((SLABS, CONV_TAPS, LANES), lambda g, c, j: (c, 0, 0)),
        pl.BlockSpec((SLABS, 1, LANES), lambda g, c, j: (c, 0, 0)),
        pl.BlockSpec((1, SLABS, 2 * LANES, 2 * LANES), lambda g, c, j: (j // nchunk, c, 0, 0)),
        pl.BlockSpec((1, SLABS, 1, LANES), lambda g, c, j: (j // nchunk, c, 0, 0)),
        pl.BlockSpec((BATCH_GROUP, tc, SLABS * LANES), lambda g, c, j: (g, out_chunk(j), c)),
    ]
    return pl.pallas_call(
        functools.partial(_rec_kernel, tc=tc, nchunk=nchunk),
        grid=grid,
        in_specs=in_specs,
        out_specs=pl.BlockSpec((BATCH_GROUP, tc, SLABS * LANES),
                               lambda g, c, j: (g, out_chunk(j), c)),
        out_shape=jax.ShapeDtypeStruct((B, S, LRU_WIDTH), BF16),
        scratch_shapes=[
            pltpu.VMEM((SLABS, rows_total, LANES), F32),
            pltpu.VMEM((SLABS, rows, LANES), F32),
            pltpu.VMEM((SLABS, BATCH_GROUP, LANES), F32),
        ],
        compiler_params=pltpu.CompilerParams(
            dimension_semantics=("parallel", "parallel", "arbitrary"),
            vmem_limit_bytes=VMEM_LIMIT),
        name="recurrent",
    )(xr, xr, xr, conv_w, conv_b, wg, lam, gate)


def _merge_kernel(x_ref, at_ref, rc_ref, ga_ref, gr_ref, pa_ref, pr_ref, wo_ref, n1_ref, n2_ref,
                  x1_ref, hn_ref):
    half = x_ref.shape[0] // 2
    for part in range(2):
        rows = slice(part * half, (part + 1) * half)
        ya = jnp.dot(at_ref[rows, :], pa_ref[...], preferred_element_type=F32)
        yr = jnp.dot(rc_ref[rows, :], pr_ref[...], preferred_element_type=F32)
        merged = ga_ref[rows, :].astype(F32) * ya + gr_ref[rows, :].astype(F32) * yr
        mix = jnp.dot(merged.astype(BF16), wo_ref[...], preferred_element_type=F32)
        x1 = x_ref[rows, :] + _rms(mix, n1_ref[...])
        x1_ref[rows, :] = x1
        hn_ref[rows, :] = _rms(x1, n2_ref[...]).astype(BF16)


def _merge(x, attn, rec, ga, gr, pa, pr, wo, n1, n2, *, tm):
    T = x.shape[0]

    def tok(width):
        return pl.BlockSpec((tm, width), lambda i: (i, 0))

    return pl.pallas_call(
        _merge_kernel,
        grid=(T // tm,),
        in_specs=[tok(D_MODEL), tok(Q_WIDTH), tok(LRU_WIDTH), tok(D_MODEL), tok(D_MODEL),
                  _const_spec((Q_WIDTH, D_MODEL)), _const_spec((LRU_WIDTH, D_MODEL)),
                  _const_spec((D_MODEL, D_MODEL)), _const_spec((1, D_MODEL)),
                  _const_spec((1, D_MODEL))],
        out_specs=(tok(D_MODEL), tok(D_MODEL)),
        out_shape=(jax.ShapeDtypeStruct((T, D_MODEL), F32),
                   jax.ShapeDtypeStruct((T, D_MODEL), BF16)),
        compiler_params=pltpu.CompilerParams(
            dimension_semantics=("parallel",), vmem_limit_bytes=VMEM_LIMIT),
        name="merge",
    )(x, attn, rec, ga, gr, pa, pr, wo, n1, n2)


FF_CHUNK = 256


def _ffn_kernel(x1_ref, hn_ref, wi_ref, wo_ref, n_ref, o_ref, act_ref):
    h = hn_ref[...]
    for c in range(D_FF // FF_CHUNK):
        lo = c * FF_CHUNK
        g = jnp.dot(h, wi_ref[:, lo:lo + FF_CHUNK], preferred_element_type=F32)
        u = jnp.dot(h, wi_ref[:, D_FF + lo:D_FF + lo + FF_CHUNK], preferred_element_type=F32)
        act_ref[:, lo:lo + FF_CHUNK] = (g * _sigmoid(g) * u).astype(BF16)
    f = jnp.dot(act_ref[...], wo_ref[...], preferred_element_type=F32)
    o_ref[...] = x1_ref[...] + _rms(f, n_ref[...])


def _ffn(x1, hn, wi, wo, nw, *, tm):
    T = x1.shape[0]

    def tok(width):
        return pl.BlockSpec((tm, width), lambda i: (i, 0))

    return pl.pallas_call(
        _ffn_kernel,
        grid=(T // tm,),
        in_specs=[tok(D_MODEL), tok(D_MODEL), _const_spec((D_MODEL, 2 * D_FF)),
                  _const_spec((D_FF, D_MODEL)), _const_spec((1, D_MODEL))],
        out_specs=tok(D_MODEL),
        out_shape=jax.ShapeDtypeStruct((T, D_MODEL), F32),
        scratch_shapes=[pltpu.VMEM((tm, D_FF), BF16)],
        compiler_params=pltpu.CompilerParams(
            dimension_semantics=("parallel",), vmem_limit_bytes=VMEM_LIMIT),
        name="ffn",
    )(x1, hn, wi, wo, nw)


def _rope_tables(S):
    half = ROT_DIM // 2
    inv = ROPE_THETA ** (-jnp.arange(half, dtype=F32) / half)
    ang = jnp.arange(S).astype(F32)[:, None] * inv[None, :]
    cos, sin = jnp.cos(ang), jnp.sin(ang)
    zeros = jnp.zeros((S, HEAD_DIM - ROT_DIM), F32)
    zh = jnp.zeros((S, half), F32)
    rope_c = jnp.concatenate([cos, cos, jnp.ones_like(zeros)], axis=1)
    rope_s1 = jnp.concatenate([zh, sin, zeros], axis=1)
    rope_s2 = jnp.concatenate([-sin, zh, zeros], axis=1)
    return rope_c, rope_s1, rope_s2


def _pick(S, prefer):
    return prefer if S % prefer == 0 else S


def _layer(x, p):
    B, S, _ = x.shape
    tt = _pick(S, 64)
    q, k, v, xr, rg, ga, gr = _in_proj(x, p["norm_mix_pre"], p["w_in"], *_rope_tables(S), tt=tt)
    attn = _attention(q, k, v, p["attn_sink"], tq=_pick(S, 1024))
    rec = _recurrent(xr, rg, p["conv_w"], p["conv_b"], p["wg"], p["lam"], tc=_pick(S, 256))
    T = B * S
    tm = _pick(T, 512)
    x1, hn = _merge(x.reshape(T, D_MODEL), attn.reshape(T, Q_WIDTH), rec.reshape(T, LRU_WIDTH),
                    ga.reshape(T, D_MODEL), gr.reshape(T, D_MODEL),
                    p["w_attn_proj"], p["w_rec_proj"], p["w_out"],
                    p["norm_mix_post"], p["norm_ffn_pre"], tm=tm)
    y = _ffn(x1, hn, p["w_ffn_in"], p["w_ffn_out"], p["norm_ffn_post"], tm=tm)
    return y.reshape(B, S, D_MODEL)


def _prep(norm_mix_pre, w_in, attn_sink, conv_w, conv_b, lru_w_a, lru_b_a, lru_w_i, lru_b_i,
          lru_lambda, w_attn_proj, w_rec_proj, w_out, norm_mix_post, norm_ffn_pre, w_ffn_in,
          w_ffn_out, norm_ffn_post):
    n_dirs = lru_w_a.shape[0]
    wg = jnp.concatenate([lru_w_a, lru_w_i], axis=-1).astype(BF16)
    bg = 0.5 * jnp.concatenate(
        [lru_b_a.reshape(n_dirs, LRU_BLOCKS, 1, LANES), lru_b_i.reshape(n_dirs, LRU_BLOCKS, 1, LANES)],
        axis=-1)
    bg_hi = bg.astype(BF16)
    bg_lo = (bg - bg_hi.astype(F32)).astype(BF16)
    pad = jnp.zeros((n_dirs, LRU_BLOCKS, LANES - BIAS_PARTS, 2 * LANES), BF16)
    wg = jnp.concatenate([wg, bg_hi, bg_lo, pad], axis=-2)
    return dict(
        norm_mix_pre=norm_mix_pre.reshape(1, D_MODEL),
        w_in=w_in.astype(BF16),
        attn_sink=attn_sink,
        conv_w=conv_w.reshape(CONV_TAPS, LRU_BLOCKS, LANES).transpose(1, 0, 2),
        conv_b=conv_b.reshape(LRU_BLOCKS, 1, LANES),
        wg=wg,
        lam=lru_lambda.reshape(n_dirs, LRU_BLOCKS, 1, LANES),
        w_attn_proj=w_attn_proj.astype(BF16),
        w_rec_proj=w_rec_proj.astype(BF16),
        w_out=w_out.astype(BF16),
        norm_mix_post=norm_mix_post.reshape(1, D_MODEL),
        norm_ffn_pre=norm_ffn_pre.reshape(1, D_MODEL),
        w_ffn_in=w_ffn_in.astype(BF16),
        w_ffn_out=w_ffn_out.astype(BF16),
        norm_ffn_post=norm_ffn_post.reshape(1, D_MODEL),
    )


def kernel(x_prompt, x_sample, norm_mix_pre, w_in, attn_sink, conv_w, conv_b, lru_w_a, lru_b_a,
           lru_w_i, lru_b_i, lru_lambda, w_attn_proj, w_rec_proj, w_out, norm_mix_post,
           norm_ffn_pre, w_ffn_in, w_ffn_out, norm_ffn_post):
    params = (norm_mix_pre, w_in, attn_sink, conv_w, conv_b, lru_w_a, lru_b_a, lru_w_i, lru_b_i,
              lru_lambda, w_attn_proj, w_rec_proj, w_out, norm_mix_post, norm_ffn_pre, w_ffn_in,
              w_ffn_out, norm_ffn_post)
    y_prompt, y_sample = x_prompt, x_sample
    for l in range(norm_mix_pre.shape[0]):
        p = _prep(*(a[l] for a in params))
        y_prompt = _layer(y_prompt, p)
        y_sample = _layer(y_sample, p)
    return (y_prompt, y_sample)
```
